```python
import math
import jax, jax.numpy as jnp
from jax import lax
import numpy as np

D_MODEL = 1024
BATCH = 8
SEQ = 2048
DEPTH = 2
DEC_BATCH = 128
DEC_SEQ = 8
PAST_LEN = 16384
PAGE_SIZE = 128

N_MIXERS = 2
N_A = (DEPTH + N_MIXERS - 1) // N_MIXERS
N_B = DEPTH // N_MIXERS
N_META = 16
EPS = 1e-6
GDN_HEADS = 8
GDN_DK = 128
GDN_DV = 128
GDN_KEY = GDN_HEADS * GDN_DK
GDN_VAL = GDN_HEADS * GDN_DV
GDN_CONV_CH = 2 * GDN_KEY + GDN_VAL
GDN_PROJ = GDN_CONV_CH + GDN_VAL + 2 * GDN_HEADS
GDN_CONV_W = 4
GDN_CHUNK = 64
POOL_WINDOWS = (2, 4, 8, 16)
POOL_GROUPS = 4
POOL_GW = D_MODEL // POOL_GROUPS
POOL_BUF = max(POOL_WINDOWS) - 1
D_FF = 2816
FFN_CONV_W = 3

kernel_name = 'gdn_pool_convffn_hybrid_step'


def rmsnorm(x, w):
    xf = x.astype(jnp.float32)
    y = xf * lax.rsqrt(jnp.mean(xf * xf, axis=-1, keepdims=True) + EPS)
    return (y * w.astype(jnp.float32)).astype(x.dtype)


def l2norm(x):
    return x * lax.rsqrt(jnp.sum(x * x, axis=-1, keepdims=True) + EPS)


def causal_dwconv(prev, x, w):
    width = w.shape[0]
    L = x.shape[1]
    xcat = jnp.concatenate([prev.astype(x.dtype), x], axis=1)
    out = w[0].astype(x.dtype) * xcat[:, :L]
    for j in range(1, width):
        out = out + w[j].astype(x.dtype) * xcat[:, j:j + L]
    return out, xcat[:, L:]


def _to_blocks(t, pad, chunk):
    t = jnp.pad(t, [(0, 0), (pad, 0)] + [(0, 0)] * (t.ndim - 2))
    t = t.reshape((t.shape[0], t.shape[1] // chunk, chunk) + t.shape[2:])
    return jnp.moveaxis(t, 3, 1)


def gated_delta_rule(q, k, v, g, beta, S0, chunk):
    B, L, H, _ = q.shape
    DV = v.shape[-1]
    pad = (-L) % chunk
    q, k, v, g, beta = [_to_blocks(t.astype(jnp.float32), pad, chunk) for t in (q, k, v, g, beta)]
    G = jnp.cumsum(g, axis=-1)
    idx = jnp.arange(chunk)
    causal = idx[:, None] >= idx[None, :]
    strict = idx[:, None] > idx[None, :]
    decay = jnp.exp(jnp.where(causal, G[..., :, None] - G[..., None, :], -jnp.inf))
    kk = jnp.einsum('bhnid,bhnjd->bhnij', k, k)
    lower = jnp.where(strict, beta[..., :, None] * kk * decay, 0.0)
    eye = jnp.eye(chunk, dtype=jnp.float32)
    T = lax.linalg.triangular_solve(lower + eye, jnp.broadcast_to(eye, lower.shape),
                                    left_side=True, lower=True, unit_diagonal=True)
    U = jnp.einsum('bhnij,bhnje->bhnie', T, v * beta[..., None])
    W = jnp.einsum('bhnij,bhnjd->bhnid', T, k * (beta * jnp.exp(G))[..., None])
    A = jnp.einsum('bhnid,bhnjd->bhnij', q, k) * decay
    q_dec = q * jnp.exp(G)[..., None]
    k_dec = k * jnp.exp(G[..., -1:] - G)[..., None]
    g_tot = jnp.exp(G[..., -1])
    blocks = tuple(jnp.moveaxis(t, 2, 0) for t in (U, W, A, q_dec, k_dec, g_tot))

    def step(S, blk):
        U_c, W_c, A_c, qd_c, kd_c, gt_c = blk
        v_new = U_c - jnp.einsum('bhid,bhde->bhie', W_c, S)
        o = jnp.einsum('bhid,bhde->bhie', qd_c, S) + jnp.einsum('bhij,bhje->bhie', A_c, v_new)
        S = S * gt_c[..., None, None] + jnp.einsum('bhid,bhie->bhde', kd_c, v_new)
        return S, o

    S, o = lax.scan(step, S0.astype(jnp.float32), blocks)
    o = jnp.transpose(o, (1, 0, 3, 2, 4)).reshape(B, -1, H, DV)[:, pad:]
    return o, S


def gdn_mixer(h, conv_prev, S0, w_in, conv_w, A_log, dt_bias, norm_w, w_out, chunk):
    B, L, _ = h.shape
    proj = jnp.einsum('bld,de->ble', h, w_in)
    qkv = proj[..., :GDN_CONV_CH]
    z = proj[..., GDN_CONV_CH:GDN_CONV_CH + GDN_VAL]
    b_logit = proj[..., GDN_CONV_CH + GDN_VAL:GDN_CONV_CH + GDN_VAL + GDN_HEADS]
    a_logit = proj[..., GDN_CONV_CH + GDN_VAL + GDN_HEADS:]
    qkv_c, conv_new = causal_dwconv(conv_prev, qkv, conv_w)
    qkv_c = jax.nn.silu(qkv_c.astype(jnp.float32))
    q = qkv_c[..., :GDN_KEY].reshape(B, L, GDN_HEADS, GDN_DK)
    k = qkv_c[..., GDN_KEY:2 * GDN_KEY].reshape(B, L, GDN_HEADS, GDN_DK)
    v = qkv_c[..., 2 * GDN_KEY:].reshape(B, L, GDN_HEADS, GDN_DV)
    q = l2norm(q) * (GDN_DK ** -0.5)
    k = l2norm(k)
    beta = jax.nn.sigmoid(b_logit.astype(jnp.float32))
    g = -jnp.exp(A_log.astype(jnp.float32)) * jax.nn.softplus(
        a_logit.astype(jnp.float32) + dt_bias.astype(jnp.float32))
    o, S = gated_delta_rule(q, k, v, g, beta, S0, chunk)
    o = o * lax.rsqrt(jnp.mean(o * o, axis=-1, keepdims=True) + EPS) * norm_w.astype(jnp.float32)
    o = o * jax.nn.silu(z.astype(jnp.float32)).reshape(B, L, GDN_HEADS, GDN_DV)
    out = jnp.einsum('ble,ed->bld', o.reshape(B, L, GDN_VAL).astype(h.dtype), w_out)
    return out, conv_new, S


def pool_mixer(h, prev, start_pos, w_grp, scale):
    B, L, _ = h.shape
    hcat = jnp.concatenate([prev.astype(h.dtype), h], axis=1)
    csum = jnp.cumsum(hcat.astype(jnp.float32), axis=1)
    csum = jnp.concatenate([jnp.zeros((B, 1, D_MODEL), jnp.float32), csum], axis=1)
    off = POOL_BUF + 1
    pos = start_pos + jnp.arange(L)
    hf = h.astype(jnp.float32)
    outs = []
    for gi, win in enumerate(POOL_WINDOWS):
        lo, hi = gi * POOL_GW, (gi + 1) * POOL_GW
        wsum = csum[:, off:off + L, lo:hi] - csum[:, off - win:off - win + L, lo:hi]
        cnt = jnp.minimum(win, pos + 1).astype(jnp.float32)[None, :, None]
        pooled = wsum / cnt - hf[..., lo:hi]
        outs.append(jnp.einsum('blc,ce->ble', pooled, w_grp[gi].astype(jnp.float32)))
    out = jnp.concatenate(outs, axis=-1) * scale.astype(jnp.float32)
    return out.astype(h.dtype), hcat[:, L:]


def conv_ffn(h, prev, w_up, conv_w, conv_b, w_down):
    u = jnp.einsum('bld,df->blf', h, w_up)
    uc, buf = causal_dwconv(prev, u, conv_w)
    uc = uc + conv_b.astype(uc.dtype)
    a, b = uc[..., :D_FF], uc[..., D_FF:]
    return jnp.einsum('blf,fd->bld', jax.nn.silu(a) * b, w_down), buf


def trunk(x, gdn_conv_prev, gdn_S0, pool_prev, ffn_prev, start_pos, chunk,
          norm_mix, norm_ffn, gdn_w_in, gdn_conv_w, gdn_A_log, gdn_dt_bias, gdn_norm_w, gdn_w_out,
          pool_w, pool_scale, ffn_w_up, ffn_conv_w, ffn_conv_b, ffn_w_down, norm_final):
    conv_new, S_new, pool_new, ffn_new = [], [], [], []
    for i in range(DEPTH):
        j = i // N_MIXERS
        h = rmsnorm(x, norm_mix[i])
        if i % N_MIXERS == 0:
            m, c, S = gdn_mixer(h, gdn_conv_prev[j], gdn_S0[j], gdn_w_in[j], gdn_conv_w[j],
                                gdn_A_log[j], gdn_dt_bias[j], gdn_norm_w[j], gdn_w_out[j], chunk)
            conv_new.append(c)
            S_new.append(S)
        else:
            m, pbuf = pool_mixer(h, pool_prev[j], start_pos, pool_w[j], pool_scale[j])
            pool_new.append(pbuf)
        x = x + m.astype(x.dtype)
        h = rmsnorm(x, norm_ffn[i])
        f, fbuf = conv_ffn(h, ffn_prev[i], ffn_w_up[i], ffn_conv_w[i], ffn_conv_b[i], ffn_w_down[i])
        ffn_new.append(fbuf)
        x = x + f.astype(x.dtype)
    y = rmsnorm(x, norm_final)
    return y, jnp.stack(conv_new), jnp.stack(S_new), jnp.stack(pool_new), jnp.stack(ffn_new)


def setup_inputs(seed: int = 0) -> dict:
    key = jax.random.key(seed)
    ks = jax.random.split(key, 22)

    def nrm(k, shape, s):
        return jax.random.normal(k, shape, jnp.float32) * s

    dt = jnp.exp(jax.random.uniform(ks[12], (N_A, GDN_HEADS), jnp.float32,
                                    minval=math.log(1e-3), maxval=math.log(1e-1)))
    return {
        'x_prompt': nrm(ks[0], (BATCH, SEQ, D_MODEL), 1.0),
        'x_sample': nrm(ks[1], (DEC_BATCH, DEC_SEQ, D_MODEL), 1.0),
        'state_gdn_conv': nrm(ks[2], (N_A, DEC_BATCH, GDN_CONV_W - 1, GDN_CONV_CH), 1.0),
        'state_gdn_rec': nrm(ks[3], (N_A, DEC_BATCH, GDN_HEADS, GDN_DK, GDN_DV), GDN_DK ** -0.5),
        'state_pool': nrm(ks[4], (N_B, DEC_BATCH, POOL_BUF, D_MODEL), 1.0),
        'state_ffn_conv': nrm(ks[5], (DEPTH, DEC_BATCH, FFN_CONV_W - 1, 2 * D_FF), 1.0),
        'meta_tokens': nrm(ks[6], (N_META, D_MODEL), 1.0),
        'norm_mix': 1.0 + nrm(ks[7], (DEPTH, D_MODEL), 0.02),
        'norm_ffn': 1.0 + nrm(ks[8], (DEPTH, D_MODEL), 0.02),
        'gdn_w_in': nrm(ks[9], (N_A, D_MODEL, GDN_PROJ), D_MODEL ** -0.5),
        'gdn_conv_w': nrm(ks[10], (N_A, GDN_CONV_W, GDN_CONV_CH), GDN_CONV_W ** -0.5),
        'gdn_A_log': jnp.log(jax.random.uniform(ks[11], (N_A, GDN_HEADS), jnp.float32, minval=1.0, maxval=16.0)),
        'gdn_dt_bias': dt + jnp.log(-jnp.expm1(-dt)),
        'gdn_norm_w': 1.0 + nrm(ks[13], (N_A, GDN_DV), 0.02),
        'gdn_w_out': nrm(ks[14], (N_A, GDN_VAL, D_MODEL), GDN_VAL ** -0.5),
        'pool_w': nrm(ks[15], (N_B, POOL_GROUPS, POOL_GW, POOL_GW), POOL_GW ** -0.5),
        'pool_scale': 1.0 + nrm(ks[16], (N_B, D_MODEL), 0.05),
        'ffn_w_up': nrm(ks[17], (DEPTH, D_MODEL, 2 * D_FF), D_MODEL ** -0.5),
        'ffn_conv_w': nrm(ks[18], (DEPTH, FFN_CONV_W, 2 * D_FF), FFN_CONV_W ** -0.5),
        'ffn_conv_b': nrm(ks[19], (DEPTH, 2 * D_FF), 0.01),
        'ffn_w_down': nrm(ks[20], (DEPTH, D_FF, D_MODEL), D_FF ** -0.5),
        'norm_final': 1.0 + nrm(ks[21], (D_MODEL,), 0.02),
    }


def reference(x_prompt, x_sample, state_gdn_conv, state_gdn_rec, state_pool, state_ffn_conv,
              meta_tokens, norm_mix, norm_ffn, gdn_w_in, gdn_conv_w, gdn_A_log, gdn_dt_bias,
              gdn_norm_w, gdn_w_out, pool_w, pool_scale, ffn_w_up, ffn_conv_w, ffn_conv_b,
              ffn_w_down, norm_final):
    Bp = x_prompt.shape[0]
    dtp = x_prompt.dtype
    meta = jnp.broadcast_to(meta_tokens.astype(dtp)[None], (Bp, N_META, D_MODEL))
    xp = jnp.concatenate([meta, x_prompt], axis=1)
    zc = jnp.zeros((N_A, Bp, GDN_CONV_W - 1, GDN_CONV_CH), dtp)
    zS = jnp.zeros((N_A, Bp, GDN_HEADS, GDN_DK, GDN_DV), jnp.float32)
    zp = jnp.zeros((N_B, Bp, POOL_BUF, D_MODEL), dtp)
    zf = jnp.zeros((DEPTH, Bp, FFN_CONV_W - 1, 2 * D_FF), dtp)
    yp, p_conv, p_rec, p_pool, p_ffn = trunk(
        xp, zc, zS, zp, zf, 0, GDN_CHUNK,
        norm_mix, norm_ffn, gdn_w_in, gdn_conv_w, gdn_A_log, gdn_dt_bias, gdn_norm_w, gdn_w_out,
        pool_w, pool_scale, ffn_w_up, ffn_conv_w, ffn_conv_b, ffn_w_down, norm_final)
    y_sample, s_conv, s_rec, s_pool, s_ffn = trunk(
        x_sample, state_gdn_conv, state_gdn_rec, state_pool, state_ffn_conv, PAST_LEN,
        min(GDN_CHUNK, x_sample.shape[1]),
        norm_mix, norm_ffn, gdn_w_in, gdn_conv_w, gdn_A_log, gdn_dt_bias, gdn_norm_w, gdn_w_out,
        pool_w, pool_scale, ffn_w_up, ffn_conv_w, ffn_conv_b, ffn_w_down, norm_final)
    y_prompt = yp[:, N_META:]
    return (y_prompt, y_sample, p_conv, p_rec, p_pool, p_ffn, s_conv, s_rec, s_pool, s_ffn)
```

```python
import functools

import jax
import jax.numpy as jnp
from jax import lax
from jax.experimental import pallas as pl
from jax.experimental.pallas import tpu as pltpu

D_MODEL = 1024
N_META = 16
EPS = 1e-6
HEADS = 8
DK = 128
DV = 128
KEY = HEADS * DK
VAL = HEADS * DV
QKV = 2 * KEY + VAL
CONV_W = 4
GDN_CHUNK = 64
POOL_WINDOWS = (2, 4, 8, 16)
POOL_GW = D_MODEL // 4
POOL_BUF = 15
D_FF = 2816
FFN_CONV_W = 3
PAST_LEN = 16384

SUB = 8
LANE = 128
FFN_COLS = 256
ROW_TILE = 512
SHORT_ROW_TILE = 256
VMEM_LIMIT = 56 * 1024 * 1024

F32 = jnp.float32
BF16 = jnp.bfloat16
HI = lax.Precision.HIGHEST


def _dot(a, b):
    return jnp.dot(a, b, preferred_element_type=F32)


def _dot_nt(a, b):
    return lax.dot_general(a, b, (((1,), (1,)), ((), ())), preferred_element_type=F32)


def _dot_tn(a, b):
    return lax.dot_general(a, b, (((0,), (0,)), ((), ())), preferred_element_type=F32)


def _rms(x, w):
    return x * lax.rsqrt(jnp.mean(x * x, axis=-1, keepdims=True) + EPS) * w


def _silu(x):
    return x * jax.nn.sigmoid(x)


def _softplus(x):
    return jnp.maximum(x, 0.0) + jnp.log1p(jnp.exp(-jnp.abs(x)))


def _conv_taps(u, prev, width, nseq):
    rows = u.shape[0]
    xcat = jnp.concatenate([prev, u], axis=0)
    if nseq:
        return [xcat[j * nseq:j * nseq + rows] for j in range(width)], xcat[rows:]
    taps = [pltpu.roll(xcat, width - 1 - j, 0)[SUB:] for j in range(width - 1)] + [u]
    return taps, u[rows - SUB:]


class _Layout:
    def __init__(self, nseq, nt):
        self.nseq = nseq
        self.nt = nt

    def load(self, ref, cols=slice(None)):
        if self.nseq:
            return jnp.concatenate([ref[:, t, cols] for t in range(SUB)], axis=0)
        return ref[:, cols]

    def store(self, ref, val, cols=slice(None)):
        if self.nseq:
            for t in range(SUB):
                ref[:, t, cols] = val[t * self.nseq:(t + 1) * self.nseq]
        else:
            ref[:, cols] = val

    def load_state(self, ref, cols=slice(None)):
        return jnp.concatenate([ref[i, :, cols] for i in range(ref.shape[0])], axis=0)

    def store_state(self, ref, val, cols=slice(None)):
        for i in range(ref.shape[0]):
            ref[i, :, cols] = val[i * self.nseq:(i + 1) * self.nseq]

    def init_carry(self, carry_ref, prev_ref):
        if not self.nseq and self.nt > 1:
            @pl.when(pl.program_id(1) == 0)
            def _():
                carry_ref[...] = prev_ref[...]

    def conv(self, u, w_ref, prev_ref, carry_ref, new_ref, cols, width):
        if self.nseq:
            prev = self.load_state(prev_ref, cols)
        else:
            prev = (carry_ref if self.nt > 1 else prev_ref)[:, cols]
        taps, new = _conv_taps(u, prev, width, self.nseq)
        out = w_ref[0:1, cols] * taps[0]
        for j in range(1, width):
            out = out + w_ref[j:j + 1, cols] * taps[j]
        if self.nseq:
            self.store_state(new_ref, new, cols)
        else:
            new_ref[:, cols] = new
            if self.nt > 1:
                carry_ref[:, cols] = new
        return out

    def row_spec(self, rows, c):
        if self.nseq:
            return pl.BlockSpec((self.nseq, SUB, c), lambda b, t: (b, 0, 0))
        nt = self.nt
        return pl.BlockSpec((rows, c), lambda b, t: (b * nt + t, 0))

    def row_shape(self, n, c):
        return jax.ShapeDtypeStruct((n // SUB, SUB, c) if self.nseq else (n, c), F32)

    def state_spec(self, arr, nb):
        if self.nseq:
            return pl.BlockSpec((arr.shape[0], self.nseq, arr.shape[2]), lambda b, t: (0, b, 0))
        return pl.BlockSpec((arr.shape[0] // nb, arr.shape[1]), lambda b, t: (b, 0))


def _proj_kernel(x_ref, prev_ref, nw_ref, wqkv_ref, wz_ref, wg_ref, cw_ref, alog_ref, dtb_ref,
                 q_ref, k_ref, v_ref, z_ref, g_ref, cnew_ref, carry_ref, *, lay):
    h = _rms(lay.load(x_ref), nw_ref[...]).astype(BF16)
    lay.store(z_ref, _dot(h, wz_ref[...]))
    lg = _dot(h, wg_ref[...])
    beta = jax.nn.sigmoid(lg)
    g = -jnp.exp(alog_ref[...]) * _softplus(lg + dtb_ref[...])
    lane = lax.broadcasted_iota(jnp.int32, lg.shape, 1)
    lay.store(g_ref, jnp.where(lane < HEADS, beta, g))

    lay.init_carry(carry_ref, prev_ref)
    for part, out_ref in enumerate((q_ref, k_ref, v_ref)):
        cols = slice(part * KEY, (part + 1) * KEY)
        raw = _dot(h, wqkv_ref[:, cols])
        act = _silu(lay.conv(raw, cw_ref, prev_ref, carry_ref, cnew_ref, cols, CONV_W))
        if part == 2:
            lay.store(out_ref, act)
            continue
        for hd in range(HEADS):
            hs = slice(hd * DK, (hd + 1) * DK)
            a = act[:, hs]
            n = a * lax.rsqrt(jnp.sum(a * a, axis=-1, keepdims=True) + EPS)
            lay.store(out_ref, n * (DK ** -0.5) if part == 0 else n, hs)


def _const_spec(shape):
    nd = len(shape)
    return pl.BlockSpec(shape, lambda b, t: (0,) * nd, pipeline_mode=pl.Buffered(1))


_PARAMS_2D = pltpu.CompilerParams(dimension_semantics=("arbitrary", "arbitrary"), vmem_limit_bytes=VMEM_LIMIT)


def _proj(x, prev, p, *, n, nb, nt, rows, lay):
    row_spec = functools.partial(lay.row_spec, rows)
    row_shape = functools.partial(lay.row_shape, n)
    return pl.pallas_call(
        functools.partial(_proj_kernel, lay=lay),
        grid=(nb, nt),
        in_specs=[row_spec(D_MODEL), lay.state_spec(prev, nb), _const_spec((1, D_MODEL)),
                  _const_spec((D_MODEL, QKV)), _const_spec((D_MODEL, VAL)), _const_spec((D_MODEL, LANE)),
                  _const_spec((CONV_W, QKV)), _const_spec((1, LANE)), _const_spec((1, LANE))],
        out_specs=[row_spec(KEY), row_spec(KEY), row_spec(VAL), row_spec(VAL), row_spec(LANE),
                   lay.state_spec(prev, nb)],
        out_shape=[row_shape(KEY), row_shape(KEY), row_shape(VAL), row_shape(VAL), row_shape(LANE),
                   jax.ShapeDtypeStruct(prev.shape, F32)],
        scratch_shapes=[pltpu.VMEM((SUB, QKV), F32)],
        compiler_params=_PARAMS_2D,
        name="gdn_proj",
    )(x, prev, p["norm_mix0"], p["w_qkv"], p["w_z"], p["w_g"], p["gdn_conv_w"], p["a_log"], p["dt_bias"])


def _unit_lower_inverse(low, eye, n_sq):
    x = eye - low
    p = low
    for _ in range(n_sq):
        p = jnp.dot(p, p, precision=HI, preferred_element_type=F32)
        x = x + jnp.dot(x, p, precision=HI, preferred_element_type=F32)
    return x


def _chunk_terms(q, k, v, beta_b, g_b, same, n_sq):
    c = q.shape[0]
    ri = lax.broadcasted_iota(jnp.int32, (c, c), 0)
    ci = lax.broadcasted_iota(jnp.int32, (c, c), 1)
    eye_b = ri == ci
    eye = eye_b.astype(F32)
    g_row = jnp.sum(jnp.where(eye_b, g_b, 0.0), axis=0, keepdims=True)
    cum_row = jnp.sum(jnp.where(same & (ri <= ci), g_b, 0.0), axis=0, keepdims=True)
    cum_col = jnp.sum(jnp.where(eye_b, cum_row, 0.0), axis=1, keepdims=True)
    g_tot = jnp.sum(jnp.where(same, g_row, 0.0), axis=1, keepdims=True)
    causal = same & (ri >= ci)
    decay = jnp.exp(jnp.where(causal, cum_col - cum_row, -jnp.inf))
    e_col = jnp.exp(cum_col)
    kk = _dot_nt(k, k)
    low = jnp.where(same & (ri > ci), beta_b * kk * decay, 0.0)
    t_inv = _unit_lower_inverse(low, eye, n_sq)
    beta_col = beta_b[:, 0:1]
    u = _dot(t_inv, v * beta_col)
    w = _dot(t_inv, k * (beta_col * e_col))
    a = _dot_nt(q, k) * decay
    q_dec = q * e_col
    k_dec = k * jnp.exp(g_tot - cum_col)
    return u, w, a, q_dec, k_dec, g_tot


def _delta_long_kernel(q_ref, k_ref, v_ref, g_ref, s0_ref, o_ref, s_ref, *, chunk, n_chunks, n_sq):
    t = pl.program_id(1)

    @pl.when(t == 0)
    def _():
        s_ref[...] = s0_ref[...]

    same = jnp.full((chunk, chunk), True)

    def body(c, carry):
        rows = pl.ds(pl.multiple_of(c * chunk, chunk), chunk)
        gates = g_ref[rows, :]
        for hd in range(HEADS):
            hs = slice(hd * DK, (hd + 1) * DK)
            beta_b = jnp.broadcast_to(gates[:, hd:hd + 1], (chunk, chunk))
            g_b = jnp.broadcast_to(gates[:, HEADS + hd:HEADS + hd + 1], (chunk, chunk))
            u, w, a, q_dec, k_dec, g_tot = _chunk_terms(q_ref[rows, hs], k_ref[rows, hs], v_ref[rows, hs],
                                                        beta_b, g_b, same, n_sq)
            s = s_ref[0, hd]
            v_new = u - _dot(w, s)
            o_ref[rows, hs] = _dot(q_dec, s) + _dot(a, v_new)
            s_ref[0, hd] = s * jnp.exp(g_tot[0:1, :]) + _dot_tn(k_dec, v_new)
        return carry

    lax.fori_loop(0, n_chunks, body, 0)


def _delta_long(q, k, v, gates, s0, *, n, nb, nt, rows, lay, chunk):
    row_spec = functools.partial(lay.row_spec, rows)
    s_spec = pl.BlockSpec((1, HEADS, DK, DV), lambda b, t: (b, 0, 0, 0))
    n_sq = chunk.bit_length() - 2
    kern = functools.partial(_delta_long_kernel, chunk=chunk, n_chunks=rows // chunk, n_sq=n_sq)
    return pl.pallas_call(
        kern,
        grid=(nb, nt),
        in_specs=[row_spec(KEY), row_spec(KEY), row_spec(VAL), row_spec(LANE), s_spec],
        out_specs=[row_spec(VAL), s_spec],
        out_shape=[jax.ShapeDtypeStruct((n, VAL), F32), jax.ShapeDtypeStruct(s0.shape, F32)],
        compiler_params=_PARAMS_2D,
        name="gdn_delta_long",
    )(q, k, v, gates, s0)


def _delta_short_kernel(q_ref, k_ref, v_ref, g_ref, s0_ref, o_ref, s_ref, *, nseq):
    c = HEADS * SUB
    ri = lax.broadcasted_iota(jnp.int32, (c, c), 0)
    ci = lax.broadcasted_iota(jnp.int32, (c, c), 1)
    same = (ri // SUB) == (ci // SUB)
    rh = lax.broadcasted_iota(jnp.int32, (c, KEY), 0) // SUB
    ch = lax.broadcasted_iota(jnp.int32, (c, KEY), 1) // DK
    head_cols = rh == ch
    for sq in range(nseq):
        rows = slice(sq * SUB, (sq + 1) * SUB)
        gates = g_ref[rows, :]
        heads_to_rows = lambda ref: jnp.concatenate(
            [ref[rows, hd * DK:(hd + 1) * DK] for hd in range(HEADS)], axis=0)
        q, k, v = heads_to_rows(q_ref), heads_to_rows(k_ref), heads_to_rows(v_ref)
        beta_b = jnp.concatenate(
            [jnp.broadcast_to(gates[:, hd:hd + 1], (SUB, c)) for hd in range(HEADS)], axis=0)
        g_b = jnp.concatenate(
            [jnp.broadcast_to(gates[:, HEADS + hd:HEADS + hd + 1], (SUB, c)) for hd in range(HEADS)], axis=0)
        u, w, a, q_dec, k_dec, g_tot = _chunk_terms(q, k, v, beta_b, g_b, same, 2)
        s = s0_ref[sq].reshape(KEY, DV)
        spread = lambda m: jnp.where(head_cols, jnp.concatenate([m] * HEADS, axis=1), 0.0)
        v_new = u - _dot(spread(w), s)
        o = _dot(spread(q_dec), s) + _dot(a, v_new)
        o_ref[rows, :] = jnp.concatenate([o[hd * SUB:(hd + 1) * SUB] for hd in range(HEADS)], axis=1)
        upd = _dot_tn(spread(k_dec), v_new)
        for hd in range(HEADS):
            ks = slice(hd * DK, (hd + 1) * DK)
            s_ref[sq, hd] = s[ks] * jnp.exp(g_tot[hd * SUB:hd * SUB + 1, :]) + upd[ks]


def _delta_short(q, k, v, gates, s0, *, nseq):
    n = q.shape[0]
    nb = n // (nseq * SUB)
    row_spec = lambda c: pl.BlockSpec((nseq * SUB, c), lambda b: (b, 0))
    s_spec = pl.BlockSpec((nseq, HEADS, DK, DV), lambda b: (b, 0, 0, 0))
    return pl.pallas_call(
        functools.partial(_delta_short_kernel, nseq=nseq),
        grid=(nb,),
        in_specs=[row_spec(KEY), row_spec(KEY), row_spec(VAL), row_spec(LANE), s_spec],
        out_specs=[row_spec(VAL), s_spec],
        out_shape=[jax.ShapeDtypeStruct((n, VAL), F32), jax.ShapeDtypeStruct(s0.shape, F32)],
        compiler_params=pltpu.CompilerParams(dimension_semantics=("arbitrary",),
                                             vmem_limit_bytes=VMEM_LIMIT),
        name="gdn_delta_short",
    )(q, k, v, gates, s0)


def _conv_ffn(h, lay, prev_ref, carry_ref, new_ref, wup_ref, cw_ref, cb_ref, wdown_ref, act_ref):
    lay.init_carry(carry_ref, prev_ref)
    for c in range(D_FF // FFN_COLS):
        halves = []
        for part in range(2):
            lo = part * D_FF + c * FFN_COLS
            cols = slice(lo, lo + FFN_COLS)
            u = _dot(h, wup_ref[:, cols])
            halves.append(lay.conv(u, cw_ref, prev_ref, carry_ref, new_ref, cols, FFN_CONV_W) + cb_ref[:, cols])
        act_ref[:, c * FFN_COLS:(c + 1) * FFN_COLS] = (_silu(halves[0]) * halves[1]).astype(BF16)
    return _dot(act_ref[...], wdown_ref[...])


def _mixffn_kernel(x_ref, o_ref, z_ref, gnw_ref, wout_ref, nf_ref, prev_ref, wup_ref, cw_ref, cb_ref, wdown_ref,
                   y_ref, new_ref, carry_ref, act_ref, gated_ref, *, lay):
    for hd in range(HEADS):
        hs = slice(hd * DV, (hd + 1) * DV)
        o = lay.load(o_ref, hs)
        o = o * lax.rsqrt(jnp.mean(o * o, axis=-1, keepdims=True) + EPS) * gnw_ref[...]
        gated_ref[:, hs] = (o * _silu(lay.load(z_ref, hs))).astype(BF16)
    x1 = lay.load(x_ref) + _dot(gated_ref[...], wout_ref[...])
    h = _rms(x1, nf_ref[...]).astype(BF16)
    f = _conv_ffn(h, lay, prev_ref, carry_ref, new_ref, wup_ref, cw_ref, cb_ref, wdown_ref, act_ref)
    lay.store(y_ref, x1 + f)


def _ffn_specs(lay, prev, nb):
    return [lay.state_spec(prev, nb), _const_spec((D_MODEL, 2 * D_FF)),
            _const_spec((FFN_CONV_W, 2 * D_FF)), _const_spec((1, 2 * D_FF)), _const_spec((D_FF, D_MODEL))]


def _mixffn(x, o, z, prev, p, *, n, nb, nt, rows, lay):
    row_spec = functools.partial(lay.row_spec, rows)
    return pl.pallas_call(
        functools.partial(_mixffn_kernel, lay=lay),
        grid=(nb, nt),
        in_specs=[row_spec(D_MODEL), row_spec(VAL), row_spec(VAL), _const_spec((1, DV)),
                  _const_spec((VAL, D_MODEL)), _const_spec((1, D_MODEL))] + _ffn_specs(lay, prev, nb),
        out_specs=[row_spec(D_MODEL), lay.state_spec(prev, nb)],
        out_shape=[lay.row_shape(n, D_MODEL), jax.ShapeDtypeStruct(prev.shape, F32)],
        scratch_shapes=[pltpu.VMEM((SUB, 2 * D_FF), F32), pltpu.VMEM((rows, D_FF), BF16),
                        pltpu.VMEM((rows, VAL), BF16)],
        compiler_params=_PARAMS_2D,
        name="gdn_out_ffn",
    )(x, o, z, p["gdn_norm_w"], p["w_out"], p["norm_ffn0"], prev, p["w_up0"], p["ffn_conv_w0"],
      p["ffn_conv_b0"], p["w_down0"])


def _window_sum(hcat, level, step):
    s = hcat
    for lv in range(level):
        d = step << lv
        s = s[d:] + s[:s.shape[0] - d]
    return s


def _poolffn_kernel(x_ref, pprev_ref, nm_ref, pw_ref, ps_ref, nf_ref, prev_ref, wup_ref, cw_ref, cb_ref, wdown_ref,
                    nfin_ref, y_ref, pnew_ref, new_ref, pcarry_ref, carry_ref, act_ref, mix_ref, *, lay, start_pos):
    x = lay.load(x_ref)
    rows = x.shape[0]
    h = _rms(x, nm_ref[...])
    row = lax.broadcasted_iota(jnp.int32, (rows, 1), 0)
    if lay.nseq:
        step = lay.nseq
        pos = start_pos + row // step
        hcat = jnp.concatenate([lay.load_state(pprev_ref), h], axis=0)
        lay.store_state(pnew_ref, hcat[rows:])
        first = POOL_BUF * step
    else:
        step = 1
        pos = start_pos + pl.program_id(1) * rows + row
        if lay.nt > 1:
            @pl.when(pl.program_id(1) == 0)
            def _():
                pcarry_ref[...] = pprev_ref[...]
            pprev = pcarry_ref[...]
        else:
            pprev = pprev_ref[...]
        hcat = jnp.concatenate([pprev, h], axis=0)
        pnew_ref[...] = hcat[rows:]
        if lay.nt > 1:
            pcarry_ref[...] = hcat[rows:]
        first = 2 * SUB
    for gi, win in enumerate(POOL_WINDOWS):
        cols = slice(gi * POOL_GW, (gi + 1) * POOL_GW)
        if lay.nseq:
            lo = first - (win - 1) * step
            wsum = _window_sum(hcat[:, cols], gi + 1, step)[lo:lo + rows]
        else:
            s = hcat[:, cols]
            for lv in range(gi + 1):
                s = s + pltpu.roll(s, 1 << lv, 0)
            wsum = s[first:]
        cnt = jnp.minimum(win, pos + 1).astype(F32)
        pooled = wsum / cnt - h[:, cols]
        mix_ref[:, cols] = _dot(pooled.astype(BF16), pw_ref[gi])
    x3 = x + mix_ref[...] * ps_ref[...]
    hf = _rms(x3, nf_ref[...]).astype(BF16)
    f = _conv_ffn(hf, lay, prev_ref, carry_ref, new_ref, wup_ref, cw_ref, cb_ref, wdown_ref, act_ref)
    lay.store(y_ref, _rms(x3 + f, nfin_ref[...]))


def _poolffn(x, pool_prev, prev, p, *, n, nb, nt, rows, lay, start_pos):
    row_spec = functools.partial(lay.row_spec, rows)
    return pl.pallas_call(
        functools.partial(_poolffn_kernel, lay=lay, start_pos=start_pos),
        grid=(nb, nt),
        in_specs=[row_spec(D_MODEL), lay.state_spec(pool_prev, nb),
                  _const_spec((1, D_MODEL)), _const_spec((4, POOL_GW, POOL_GW)), _const_spec((1, D_MODEL)),
                  _const_spec((1, D_MODEL))] + _ffn_specs(lay, prev, nb) + [_const_spec((1, D_MODEL))],
        out_specs=[row_spec(D_MODEL), lay.state_spec(pool_prev, nb), lay.state_spec(prev, nb)],
        out_shape=[lay.row_shape(n, D_MODEL), jax.ShapeDtypeStruct(pool_prev.shape, F32),
                   jax.ShapeDtypeStruct(prev.shape, F32)],
        scratch_shapes=[pltpu.VMEM((2 * SUB, D_MODEL), F32), pltpu.VMEM((SUB, 2 * D_FF), F32),
                        pltpu.VMEM((rows, D_FF), BF16), pltpu.VMEM((rows, D_MODEL), F32)],
        compiler_params=_PARAMS_2D,
        name="pool_ffn",
    )(x, pool_prev, p["norm_mix1"], p["pool_w"], p["pool_scale"], p["norm_ffn1"], prev, p["w_up1"],
      p["ffn_conv_w1"], p["ffn_conv_b1"], p["w_down1"], p["norm_final"])


def _trunk(x, n_seq, seq_len, conv_prev, s0, pool_prev, ffn_prev, start_pos, p):
    n = n_seq * seq_len
    if seq_len == SUB:
        rows = min(SHORT_ROW_TILE, n)
        geom = dict(n=n, nb=n // rows, nt=1, rows=rows, lay=_Layout(rows // SUB, 1))
    else:
        rows = min(ROW_TILE, seq_len)
        geom = dict(n=n, nb=n_seq, nt=seq_len // rows, rows=rows, lay=_Layout(0, seq_len // rows))
    q, k, v, z, gates, conv_new = _proj(x, conv_prev, p, **geom)
    if seq_len == SUB:
        flat = lambda a: a.reshape(n, a.shape[-1])
        o, s_new = _delta_short(flat(q), flat(k), flat(v), flat(gates), s0, nseq=1)
        o = o.reshape(n_seq, SUB, VAL)
    else:
        o, s_new = _delta_long(q, k, v, gates, s0, chunk=min(GDN_CHUNK, seq_len), **geom)
    x2, ffn_new0 = _mixffn(x, o, z, ffn_prev[0], p, **geom)
    y, pool_new, ffn_new1 = _poolffn(x2, pool_prev, ffn_prev[1], p, start_pos=start_pos, **geom)
    return y, conv_new, s_new, pool_new, (ffn_new0, ffn_new1)


def _prep_params(norm_mix, norm_ffn, gdn_w_in, gdn_conv_w, gdn_A_log, gdn_dt_bias, gdn_norm_w, gdn_w_out,
                 pool_w, pool_scale, ffn_w_up, ffn_conv_w, ffn_conv_b, ffn_w_down, norm_final):
    w_in = gdn_w_in[0]
    gate_pad = ((0, 0), (0, LANE - 2 * HEADS))
    head_pad = ((0, 0), (HEADS, LANE - 2 * HEADS))
    return {
        "norm_mix0": norm_mix[0:1], "norm_mix1": norm_mix[1:2],
        "norm_ffn0": norm_ffn[0:1], "norm_ffn1": norm_ffn[1:2],
        "w_qkv": w_in[:, :QKV].astype(BF16),
        "w_z": w_in[:, QKV:QKV + VAL].astype(BF16),
        "w_g": jnp.pad(w_in[:, QKV + VAL:], gate_pad).astype(BF16),
        "gdn_conv_w": gdn_conv_w[0],
        "a_log": jnp.pad(gdn_A_log[0:1], head_pad),
        "dt_bias": jnp.pad(gdn_dt_bias[0:1], head_pad),
        "gdn_norm_w": gdn_norm_w[0:1],
        "w_out": gdn_w_out[0].astype(BF16),
        "pool_w": pool_w[0].astype(BF16),
        "pool_scale": pool_scale[0:1],
        "w_up0": ffn_w_up[0].astype(BF16), "w_up1": ffn_w_up[1].astype(BF16),
        "ffn_conv_w0": ffn_conv_w[0], "ffn_conv_w1": ffn_conv_w[1],
        "ffn_conv_b0": ffn_conv_b[0:1], "ffn_conv_b1": ffn_conv_b[1:2],
        "w_down0": ffn_w_down[0].astype(BF16), "w_down1": ffn_w_down[1].astype(BF16),
        "norm_final": norm_final[None],
    }


def kernel(x_prompt, x_sample, state_gdn_conv, state_gdn_rec, state_pool, state_ffn_conv, meta_tokens, norm_mix, norm_ffn, gdn_w_in, gdn_conv_w, gdn_A_log, gdn_dt_bias, gdn_norm_w, gdn_w_out, pool_w, pool_scale, ffn_w_up, ffn_conv_w, ffn_conv_b, ffn_w_down, norm_final):
    p = _prep_params(norm_mix, norm_ffn, gdn_w_in, gdn_conv_w, gdn_A_log, gdn_dt_bias, gdn_norm_w, gdn_w_out,
                     pool_w, pool_scale, ffn_w_up, ffn_conv_w, ffn_conv_b, ffn_w_down, norm_final)
    bp, lp, _ = x_prompt.shape
    bs, ls, _ = x_sample.shape

    _, m_conv, m_s, m_pool, m_ffn = _trunk(
        meta_tokens.astype(F32), 1, N_META,
        jnp.zeros((SUB, QKV), F32), jnp.zeros((1, HEADS, DK, DV), F32), jnp.zeros((2 * SUB, D_MODEL), F32),
        (jnp.zeros((SUB, 2 * D_FF), F32), jnp.zeros((SUB, 2 * D_FF), F32)), 0, p)

    rep = lambda a: jnp.tile(a, (bp, 1))
    yp, p_conv, p_s, p_pool, p_ffn = _trunk(
        x_prompt.reshape(bp * lp, D_MODEL), bp, lp,
        rep(m_conv), jnp.broadcast_to(m_s, (bp, HEADS, DK, DV)), rep(m_pool), (rep(m_ffn[0]), rep(m_ffn[1])),
        N_META, p)

    tm = lambda a: jnp.swapaxes(a, 0, 1)
    ys, s_conv, s_s, s_pool, s_ffn = _trunk(
        x_sample, bs, ls,
        tm(state_gdn_conv[0]), state_gdn_rec[0], tm(state_pool[0]),
        (tm(state_ffn_conv[0]), tm(state_ffn_conv[1])), PAST_LEN, p)

    last = lambda a, b, r: a.reshape(b, -1, a.shape[-1])[:, -r:]
    return (yp.reshape(bp, lp, D_MODEL), ys,
            last(p_conv, bp, CONV_W - 1)[None], p_s[None], last(p_pool, bp, POOL_BUF)[None],
            jnp.stack([last(p_ffn[0], bp, FFN_CONV_W - 1), last(p_ffn[1], bp, FFN_CONV_W - 1)]),
            tm(s_conv)[None], s_s[None], tm(s_pool)[None],
            jnp.stack([tm(s_ffn[0]), tm(s_ffn[1])]))
```

```python
import functools

import jax
import jax.numpy as jnp
from jax import lax
from jax.experimental import pallas as pl
from jax.experimental.pallas import tpu as pltpu

D_MODEL = 1024
N_META = 16
EPS = 1e-6
HEADS = 8
DK = 128
DV = 128
KEY = HEADS * DK
VAL = HEADS * DV
QKV = 2 * KEY + VAL
CONV_W = 4
GDN_CHUNK = 64
POOL_WINDOWS = (2, 4, 8, 16)
POOL_GW = D_MODEL // 4
POOL_BUF = 15
D_FF = 2816
FFN_CONV_W = 3
PAST_LEN = 16384

SUB = 8
LANE = 128
FFN_COLS = 256
ROW_TILE = 512
SHORT_ROW_TILE = 256
VMEM_LIMIT = 56 * 1024 * 1024

F32 = jnp.float32
BF16 = jnp.bfloat16
HI = lax.Precision.HIGHEST


def _dot(a, b):
    return jnp.dot(a, b, preferred_element_type=F32)


def _dot_nt(a, b):
    return lax.dot_general(a, b, (((1,), (1,)), ((), ())), preferred_element_type=F32)


def _dot_tn(a, b):
    return lax.dot_general(a, b, (((0,), (0,)), ((), ())), preferred_element_type=F32)


def _rms(x, w):
    return x * lax.rsqrt(jnp.mean(x * x, axis=-1, keepdims=True) + EPS) * w


def _silu(x):
    return x * jax.nn.sigmoid(x)


def _softplus(x):
    return jnp.maximum(x, 0.0) + jnp.log1p(jnp.exp(-jnp.abs(x)))


def _conv_taps(u, prev, width, nseq):
    rows = u.shape[0]
    xcat = jnp.concatenate([prev, u], axis=0)
    if nseq:
        return [xcat[j * nseq:j * nseq + rows] for j in range(width)], xcat[rows:]
    taps = [pltpu.roll(xcat, width - 1 - j, 0)[SUB:] for j in range(width - 1)] + [u]
    return taps, u[rows - SUB:]


class _Layout:
    def __init__(self, nseq, nt):
        self.nseq = nseq
        self.nt = nt

    def load(self, ref, cols=slice(None)):
        if self.nseq:
            return jnp.concatenate([ref[:, t, cols] for t in range(SUB)], axis=0)
        return ref[:, cols]

    def store(self, ref, val, cols=slice(None)):
        val = val.astype(ref.dtype)
        if self.nseq:
            for t in range(SUB):
                ref[:, t, cols] = val[t * self.nseq:(t + 1) * self.nseq]
        else:
            ref[:, cols] = val

    def load_state(self, ref, cols=slice(None)):
        return jnp.concatenate([ref[i, :, cols] for i in range(ref.shape[0])], axis=0)

    def store_state(self, ref, val, cols=slice(None)):
        for i in range(ref.shape[0]):
            ref[i, :, cols] = val[i * self.nseq:(i + 1) * self.nseq]

    def init_carry(self, carry_ref, prev_ref):
        if not self.nseq and self.nt > 1:
            @pl.when(pl.program_id(1) == 0)
            def _():
                carry_ref[...] = prev_ref[...]

    def conv(self, u, w_ref, prev_ref, carry_ref, new_ref, cols, width):
        if self.nseq:
            prev = self.load_state(prev_ref, cols)
        else:
            prev = (carry_ref if self.nt > 1 else prev_ref)[:, cols]
        taps, new = _conv_taps(u, prev, width, self.nseq)
        out = w_ref[0:1, cols] * taps[0]
        for j in range(1, width):
            out = out + w_ref[j:j + 1, cols] * taps[j]
        if self.nseq:
            self.store_state(new_ref, new, cols)
        else:
            new_ref[:, cols] = new
            if self.nt > 1:
                carry_ref[:, cols] = new
        return out

    def row_spec(self, rows, c):
        if self.nseq:
            return pl.BlockSpec((self.nseq, SUB, c), lambda b, t: (b, 0, 0))
        nt = self.nt
        return pl.BlockSpec((rows, c), lambda b, t: (b * nt + t, 0))

    def row_shape(self, n, c, dtype=F32):
        return jax.ShapeDtypeStruct((n // SUB, SUB, c) if self.nseq else (n, c), dtype)

    def state_spec(self, arr, nb):
        if self.nseq:
            return pl.BlockSpec((arr.shape[0], self.nseq, arr.shape[2]), lambda b, t: (0, b, 0))
        return pl.BlockSpec((arr.shape[0] // nb, arr.shape[1]), lambda b, t: (b, 0))


def _proj_kernel(x_ref, prev_ref, nw_ref, wqkv_ref, wz_ref, wg_ref, cw_ref, alog_ref, dtb_ref,
                 q_ref, k_ref, v_ref, z_ref, g_ref, cnew_ref, carry_ref, *, lay):
    h = _rms(lay.load(x_ref), nw_ref[...]).astype(BF16)
    lay.store(z_ref, _dot(h, wz_ref[...]))
    lg = _dot(h, wg_ref[...])
    beta = jax.nn.sigmoid(lg)
    g = -jnp.exp(alog_ref[...]) * _softplus(lg + dtb_ref[...])
    lane = lax.broadcasted_iota(jnp.int32, lg.shape, 1)
    lay.store(g_ref, jnp.where(lane < HEADS, beta, g))

    lay.init_carry(carry_ref, prev_ref)
    for part, out_ref in enumerate((q_ref, k_ref, v_ref)):
        cols = slice(part * KEY, (part + 1) * KEY)
        raw = _dot(h, wqkv_ref[:, cols])
        act = _silu(lay.conv(raw, cw_ref, prev_ref, carry_ref, cnew_ref, cols, CONV_W))
        if part == 2:
            lay.store(out_ref, act)
            continue
        for hd in range(HEADS):
            hs = slice(hd * DK, (hd + 1) * DK)
            a = act[:, hs]
            n = a * lax.rsqrt(jnp.sum(a * a, axis=-1, keepdims=True) + EPS)
            lay.store(out_ref, n * (DK ** -0.5) if part == 0 else n, hs)


def _const_spec(shape):
    nd = len(shape)
    return pl.BlockSpec(shape, lambda b, t: (0,) * nd, pipeline_mode=pl.Buffered(1))


_PARAMS_2D = pltpu.CompilerParams(dimension_semantics=("arbitrary", "arbitrary"), vmem_limit_bytes=VMEM_LIMIT)


def _proj(x, prev, p, qkv_dtype, *, n, nb, nt, rows, lay):
    row_spec = functools.partial(lay.row_spec, rows)
    row_shape = functools.partial(lay.row_shape, n)
    qkv_shape = functools.partial(lay.row_shape, n, dtype=qkv_dtype)
    return pl.pallas_call(
        functools.partial(_proj_kernel, lay=lay),
        grid=(nb, nt),
        in_specs=[row_spec(D_MODEL), lay.state_spec(prev, nb), _const_spec((1, D_MODEL)),
                  _const_spec((D_MODEL, QKV)), _const_spec((D_MODEL, VAL)), _const_spec((D_MODEL, LANE)),
                  _const_spec((CONV_W, QKV)), _const_spec((1, LANE)), _const_spec((1, LANE))],
        out_specs=[row_spec(KEY), row_spec(KEY), row_spec(VAL), row_spec(VAL), row_spec(LANE),
                   lay.state_spec(prev, nb)],
        out_shape=[qkv_shape(KEY), qkv_shape(KEY), qkv_shape(VAL), row_shape(VAL), row_shape(LANE),
                   jax.ShapeDtypeStruct(prev.shape, F32)],
        scratch_shapes=[pltpu.VMEM((SUB, QKV), F32)],
        compiler_params=_PARAMS_2D,
        name="gdn_proj",
    )(x, prev, p["norm_mix0"], p["w_qkv"], p["w_z"], p["w_g"], p["gdn_conv_w"], p["a_log"], p["dt_bias"])


def _unit_lower_inverse(low, eye, n_sq):
    x = eye - low
    p = low
    for _ in range(n_sq):
        p = jnp.dot(p, p, precision=HI, preferred_element_type=F32)
        x = x + jnp.dot(x, p, precision=HI, preferred_element_type=F32)
    return x


def _chunk_terms(q, k, v, beta_b, g_b, same, n_sq):
    c = q.shape[0]
    ri = lax.broadcasted_iota(jnp.int32, (c, c), 0)
    ci = lax.broadcasted_iota(jnp.int32, (c, c), 1)
    eye_b = ri == ci
    eye = eye_b.astype(F32)
    g_row = jnp.sum(jnp.where(eye_b, g_b, 0.0), axis=0, keepdims=True)
    cum_row = jnp.sum(jnp.where(same & (ri <= ci), g_b, 0.0), axis=0, keepdims=True)
    cum_col = jnp.sum(jnp.where(eye_b, cum_row, 0.0), axis=1, keepdims=True)
    g_tot = jnp.sum(jnp.where(same, g_row, 0.0), axis=1, keepdims=True)
    causal = same & (ri >= ci)
    decay = jnp.exp(jnp.where(causal, cum_col - cum_row, -jnp.inf))
    e_col = jnp.exp(cum_col)
    kk = _dot_nt(k, k)
    low = jnp.where(same & (ri > ci), beta_b * kk * decay, 0.0)
    t_inv = _unit_lower_inverse(low, eye, n_sq)
    beta_col = beta_b[:, 0:1]
    u = _dot(t_inv, v * beta_col)
    w = _dot(t_inv, k * (beta_col * e_col))
    a = _dot_nt(q, k) * decay
    q_dec = q * e_col
    k_dec = k * jnp.exp(g_tot - cum_col)
    return u, w, a, q_dec, k_dec, g_tot


def _delta_long_kernel(q_ref, k_ref, v_ref, g_ref, s0_ref, o_ref, s_ref, *, chunk, n_chunks, n_sq):
    t = pl.program_id(1)

    @pl.when(t == 0)
    def _():
        s_ref[...] = s0_ref[...]

    same = jnp.full((chunk, chunk), True)

    def body(c, carry):
        rows = pl.ds(pl.multiple_of(c * chunk, chunk), chunk)
        gates = g_ref[rows, :]
        for hd in range(HEADS):
            hs = slice(hd * DK, (hd + 1) * DK)
            beta_b = jnp.broadcast_to(gates[:, hd:hd + 1], (chunk, chunk))
            g_b = jnp.broadcast_to(gates[:, HEADS + hd:HEADS + hd + 1], (chunk, chunk))
            u, w, a, q_dec, k_dec, g_tot = _chunk_terms(q_ref[rows, hs], k_ref[rows, hs], v_ref[rows, hs],
                                                        beta_b, g_b, same, n_sq)
            s = s_ref[0, hd]
            v_new = u - _dot(w, s)
            o_ref[rows, hs] = _dot(q_dec, s) + _dot(a, v_new)
            s_ref[0, hd] = s * jnp.exp(g_tot[0:1, :]) + _dot_tn(k_dec, v_new)
        return carry

    lax.fori_loop(0, n_chunks, body, 0)


def _delta_long(q, k, v, gates, s0, *, n, nb, nt, rows, lay, chunk):
    row_spec = functools.partial(lay.row_spec, rows)
    s_spec = pl.BlockSpec((1, HEADS, DK, DV), lambda b, t: (b, 0, 0, 0))
    n_sq = chunk.bit_length() - 2
    kern = functools.partial(_delta_long_kernel, chunk=chunk, n_chunks=rows // chunk, n_sq=n_sq)
    return pl.pallas_call(
        kern,
        grid=(nb, nt),
        in_specs=[row_spec(KEY), row_spec(KEY), row_spec(VAL), row_spec(LANE), s_spec],
        out_specs=[row_spec(VAL), s_spec],
        out_shape=[jax.ShapeDtypeStruct((n, VAL), F32), jax.ShapeDtypeStruct(s0.shape, F32)],
        compiler_params=_PARAMS_2D,
        name="gdn_delta_long",
    )(q, k, v, gates, s0)


UNITS = LANE
TILE_CHUNKS = UNITS // HEADS
TILE_ROWS = TILE_CHUNKS * GDN_CHUNK


def _col(x, lane, width):
    return jnp.broadcast_to(x[:, lane:lane + 1], (x.shape[0], width))


def _delta_tile_kernel(q_ref, k_ref, v_ref, g_ref, s0_ref, o_ref, s_ref,
                       l_scr, tcol_scr, tinv_scr, u_scr, w_scr, qd_scr, kd_scr, a_scr, gt_scr):
    @pl.when(pl.program_id(1) == 0)
    def _():
        s_ref[...] = s0_ref[...]

    chunk = GDN_CHUNK
    ri = lax.broadcasted_iota(jnp.int32, (chunk, chunk), 0)
    ci = lax.broadcasted_iota(jnp.int32, (chunk, chunk), 1)
    row_id = lax.broadcasted_iota(jnp.int32, (chunk, LANE), 0)
    heads = range(HEADS)
    hcols = [slice(h * DK, (h + 1) * DK) for h in heads]

    def phase_1a(c, carry):
        rows = pl.ds(pl.multiple_of(c * chunk, chunk), chunk)
        gates = g_ref[rows, :]
        cum = gates
        d = 1
        while d < chunk:
            cum = cum + jnp.where(row_id >= d, pltpu.roll(cum, d, 0), 0.0)
            d *= 2
        tot = cum[chunk - 1:chunk, :]
        e_cum = jnp.exp(cum)
        e_rest = jnp.exp(tot - cum)
        cum_t = cum.T
        qs = [q_ref[rows, hc] for hc in hcols]
        ks = [k_ref[rows, hc] for hc in hcols]
        sc = [_dot_nt(jnp.concatenate([qs[h], ks[h]], axis=0), ks[h]) for h in heads]
        for h in heads:
            diff = _col(cum, HEADS + h, chunk) - jnp.broadcast_to(cum_t[HEADS + h:HEADS + h + 1, :], (chunk, chunk))
            decay = jnp.exp(jnp.where(ri >= ci, diff, -jnp.inf))
            unit = c * HEADS + h
            l_scr[unit] = jnp.where(ri > ci, _col(gates, h, chunk) * sc[h][chunk:] * decay, 0.0)
            a_scr[unit] = (sc[h][:chunk] * decay).astype(BF16)
            e1 = _col(e_cum, HEADS + h, DK)
            kf = ks[h].astype(F32)
            w_scr[rows, hcols[h]] = (kf * e1).astype(BF16)
            qd_scr[rows, hcols[h]] = (qs[h].astype(F32) * e1).astype(BF16)
            kd_scr[rows, hcols[h]] = (kf * _col(e_rest, HEADS + h, DK)).astype(BF16)
        e_tot = jnp.exp(tot)
        gt_scr[pl.ds(pl.multiple_of(c * HEADS, HEADS), HEADS), :] = jnp.concatenate(
            [jnp.broadcast_to(e_tot[:, HEADS + h:HEADS + h + 1], (1, LANE)) for h in heads], axis=0)
        return carry

    lax.fori_loop(0, TILE_CHUNKS, phase_1a, 0)

    sub = lax.broadcasted_iota(jnp.int32, (SUB, LANE), 0)
    zero = jnp.zeros((SUB, LANE), F32)
    for i in range(chunk):
        nblk = i // SUB + 1
        acc = [[zero, zero] for _ in range(nblk)]
        if i:
            low_t = l_scr[:, i, :].T
        for j in range(i):
            lrow = jnp.broadcast_to(low_t[j:j + 1, :], (SUB, LANE))
            for cb in range(j // SUB + 1):
                acc[cb][j % 2] = acc[cb][j % 2] - lrow * tcol_scr[j, cb * SUB:(cb + 1) * SUB, :]
        for cb in range(chunk // SUB):
            if cb < nblk:
                blk = acc[cb][0] + acc[cb][1]
                if cb == i // SUB:
                    blk = blk + (sub == i % SUB).astype(F32)
            else:
                blk = zero
            tcol_scr[i, cb * SUB:(cb + 1) * SUB, :] = blk
    for i in range(chunk):
        tinv_scr[:, i, :] = tcol_scr[i].T

    def phase_1c(c, carry):
        rows = pl.ds(pl.multiple_of(c * chunk, chunk), chunk)
        gates_t = g_ref[rows, :].T
        tb = [(tinv_scr[c * HEADS + h] * jnp.broadcast_to(gates_t[h:h + 1, :], (chunk, chunk))).astype(BF16)
              for h in heads]
        uw = [_dot(tb[h], jnp.concatenate([v_ref[rows, hcols[h]], w_scr[rows, hcols[h]]], axis=1)) for h in heads]
        for h in heads:
            u_scr[rows, hcols[h]] = uw[h][:, :DV]
            w_scr[rows, hcols[h]] = uw[h][:, DV:].astype(BF16)
        return carry

    lax.fori_loop(0, TILE_CHUNKS, phase_1c, 0)

    def phase_2(c, carry):
        rows = pl.ds(pl.multiple_of(c * chunk, chunk), chunk)
        s = [s_ref[0, h] for h in heads]
        m1 = [_dot(jnp.concatenate([w_scr[rows, hcols[h]], qd_scr[rows, hcols[h]]], axis=0), s[h].astype(BF16))
              for h in heads]
        vb = [(u_scr[rows, hcols[h]] - m1[h][:chunk]).astype(BF16) for h in heads]
        av = [_dot(a_scr[c * HEADS + h], vb[h]) for h in heads]
        kv = [_dot_tn(kd_scr[rows, hcols[h]], vb[h]) for h in heads]
        for h in heads:
            o_ref[rows, hcols[h]] = m1[h][chunk:] + av[h]
            s_ref[0, h] = s[h] * gt_scr[pl.ds(c * HEADS + h, 1), :] + kv[h]
        return carry

    lax.fori_loop(0, TILE_CHUNKS, phase_2, 0)


def _delta_tiles(q, k, v, gates, s0, *, nb, nt):
    n = q.shape[0]
    row_spec = lambda c: pl.BlockSpec((TILE_ROWS, c), lambda b, t: (b * nt + t, 0))
    s_spec = pl.BlockSpec((1, HEADS, DK, DV), lambda b, t: (b, 0, 0, 0))
    unit_mat = lambda dtype: pltpu.VMEM((UNITS, GDN_CHUNK, GDN_CHUNK), dtype)
    return pl.pallas_call(
        _delta_tile_kernel,
        grid=(nb, nt),
        in_specs=[row_spec(KEY), row_spec(KEY), row_spec(VAL), row_spec(LANE), s_spec],
        out_specs=[row_spec(VAL), s_spec],
        out_shape=[jax.ShapeDtypeStruct((n, VAL), F32), jax.ShapeDtypeStruct(s0.shape, F32)],
        scratch_shapes=[unit_mat(F32), pltpu.VMEM((GDN_CHUNK, GDN_CHUNK, UNITS), F32), unit_mat(F32),
                        pltpu.VMEM((TILE_ROWS, VAL), F32), pltpu.VMEM((TILE_ROWS, KEY), BF16),
                        pltpu.VMEM((TILE_ROWS, KEY), BF16), pltpu.VMEM((TILE_ROWS, KEY), BF16),
                        unit_mat(BF16), pltpu.VMEM((UNITS, LANE), F32)],
        compiler_params=_PARAMS_2D,
        name="gdn_delta_tiles",
    )(q, k, v, gates, s0)


def _delta_short_kernel(q_ref, k_ref, v_ref, g_ref, s0_ref, o_ref, s_ref, *, nseq):
    c = HEADS * SUB
    ri = lax.broadcasted_iota(jnp.int32, (c, c), 0)
    ci = lax.broadcasted_iota(jnp.int32, (c, c), 1)
    same = (ri // SUB) == (ci // SUB)
    rh = lax.broadcasted_iota(jnp.int32, (c, KEY), 0) // SUB
    ch = lax.broadcasted_iota(jnp.int32, (c, KEY), 1) // DK
    head_cols = rh == ch
    for sq in range(nseq):
        rows = slice(sq * SUB, (sq + 1) * SUB)
        gates = g_ref[rows, :]
        heads_to_rows = lambda ref: jnp.concatenate(
            [ref[rows, hd * DK:(hd + 1) * DK] for hd in range(HEADS)], axis=0)
        q, k, v = heads_to_rows(q_ref), heads_to_rows(k_ref), heads_to_rows(v_ref)
        beta_b = jnp.concatenate(
            [jnp.broadcast_to(gates[:, hd:hd + 1], (SUB, c)) for hd in range(HEADS)], axis=0)
        g_b = jnp.concatenate(
            [jnp.broadcast_to(gates[:, HEADS + hd:HEADS + hd + 1], (SUB, c)) for hd in range(HEADS)], axis=0)
        u, w, a, q_dec, k_dec, g_tot = _chunk_terms(q, k, v, beta_b, g_b, same, 2)
        s = s0_ref[sq].reshape(KEY, DV)
        spread = lambda m: jnp.where(head_cols, jnp.concatenate([m] * HEADS, axis=1), 0.0)
        v_new = u - _dot(spread(w), s)
        o = _dot(spread(q_dec), s) + _dot(a, v_new)
        o_ref[rows, :] = jnp.concatenate([o[hd * SUB:(hd + 1) * SUB] for hd in range(HEADS)], axis=1)
        upd = _dot_tn(spread(k_dec), v_new)
        for hd in range(HEADS):
            ks = slice(hd * DK, (hd + 1) * DK)
            s_ref[sq, hd] = s[ks] * jnp.exp(g_tot[hd * SUB:hd * SUB + 1, :]) + upd[ks]


def _delta_short(q, k, v, gates, s0, *, nseq):
    n = q.shape[0]
    nb = n // (nseq * SUB)
    row_spec = lambda c: pl.BlockSpec((nseq * SUB, c), lambda b: (b, 0))
    s_spec = pl.BlockSpec((nseq, HEADS, DK, DV), lambda b: (b, 0, 0, 0))
    return pl.pallas_call(
        functools.partial(_delta_short_kernel, nseq=nseq),
        grid=(nb,),
        in_specs=[row_spec(KEY), row_spec(KEY), row_spec(VAL), row_spec(LANE), s_spec],
        out_specs=[row_spec(VAL), s_spec],
        out_shape=[jax.ShapeDtypeStruct((n, VAL), F32), jax.ShapeDtypeStruct(s0.shape, F32)],
        compiler_params=pltpu.CompilerParams(dimension_semantics=("arbitrary",),
                                             vmem_limit_bytes=VMEM_LIMIT),
        name="gdn_delta_short",
    )(q, k, v, gates, s0)


def _conv_ffn(h, lay, prev_ref, carry_ref, new_ref, wup_ref, cw_ref, cb_ref, wdown_ref, act_ref):
    lay.init_carry(carry_ref, prev_ref)
    for c in range(D_FF // FFN_COLS):
        halves = []
        for part in range(2):
            lo = part * D_FF + c * FFN_COLS
            cols = slice(lo, lo + FFN_COLS)
            u = _dot(h, wup_ref[:, cols])
            halves.append(lay.conv(u, cw_ref, prev_ref, carry_ref, new_ref, cols, FFN_CONV_W) + cb_ref[:, cols])
        act_ref[:, c * FFN_COLS:(c + 1) * FFN_COLS] = (_silu(halves[0]) * halves[1]).astype(BF16)
    return _dot(act_ref[...], wdown_ref[...])


def _mixffn_kernel(x_ref, o_ref, z_ref, gnw_ref, wout_ref, nf_ref, prev_ref, wup_ref, cw_ref, cb_ref, wdown_ref,
                   y_ref, new_ref, carry_ref, act_ref, gated_ref, *, lay):
    for hd in range(HEADS):
        hs = slice(hd * DV, (hd + 1) * DV)
        o = lay.load(o_ref, hs)
        o = o * lax.rsqrt(jnp.mean(o * o, axis=-1, keepdims=True) + EPS) * gnw_ref[...]
        gated_ref[:, hs] = (o * _silu(lay.load(z_ref, hs))).astype(BF16)
    x1 = lay.load(x_ref) + _dot(gated_ref[...], wout_ref[...])
    h = _rms(x1, nf_ref[...]).astype(BF16)
    f = _conv_ffn(h, lay, prev_ref, carry_ref, new_ref, wup_ref, cw_ref, cb_ref, wdown_ref, act_ref)
    lay.store(y_ref, x1 + f)


def _ffn_specs(lay, prev, nb):
    return [lay.state_spec(prev, nb), _const_spec((D_MODEL, 2 * D_FF)),
            _const_spec((FFN_CONV_W, 2 * D_FF)), _const_spec((1, 2 * D_FF)), _const_spec((D_FF, D_MODEL))]


def _mixffn(x, o, z, prev, p, *, n, nb, nt, rows, lay):
    row_spec = functools.partial(lay.row_spec, rows)
    return pl.pallas_call(
        functools.partial(_mixffn_kernel, lay=lay),
        grid=(nb, nt),
        in_specs=[row_spec(D_MODEL), row_spec(VAL), row_spec(VAL), _const_spec((1, DV)),
                  _const_spec((VAL, D_MODEL)), _const_spec((1, D_MODEL))] + _ffn_specs(lay, prev, nb),
        out_specs=[row_spec(D_MODEL), lay.state_spec(prev, nb)],
        out_shape=[lay.row_shape(n, D_MODEL), jax.ShapeDtypeStruct(prev.shape, F32)],
        scratch_shapes=[pltpu.VMEM((SUB, 2 * D_FF), F32), pltpu.VMEM((rows, D_FF), BF16),
                        pltpu.VMEM((rows, VAL), BF16)],
        compiler_params=_PARAMS_2D,
        name="gdn_out_ffn",
    )(x, o, z, p["gdn_norm_w"], p["w_out"], p["norm_ffn0"], prev, p["w_up0"], p["ffn_conv_w0"],
      p["ffn_conv_b0"], p["w_down0"])


def _window_sum(hcat, level, step):
    s = hcat
    for lv in range(level):
        d = step << lv
        s = s[d:] + s[:s.shape[0] - d]
    return s


def _poolffn_kernel(x_ref, pprev_ref, nm_ref, pw_ref, ps_ref, nf_ref, prev_ref, wup_ref, cw_ref, cb_ref, wdown_ref,
                    nfin_ref, y_ref, pnew_ref, new_ref, pcarry_ref, carry_ref, act_ref, mix_ref, *, lay, start_pos):
    x = lay.load(x_ref)
    rows = x.shape[0]
    h = _rms(x, nm_ref[...])
    row = lax.broadcasted_iota(jnp.int32, (rows, 1), 0)
    if lay.nseq:
        step = lay.nseq
        pos = start_pos + row // step
        hcat = jnp.concatenate([lay.load_state(pprev_ref), h], axis=0)
        lay.store_state(pnew_ref, hcat[rows:])
        first = POOL_BUF * step
    else:
        step = 1
        pos = start_pos + pl.program_id(1) * rows + row
        if lay.nt > 1:
            @pl.when(pl.program_id(1) == 0)
            def _():
                pcarry_ref[...] = pprev_ref[...]
            pprev = pcarry_ref[...]
        else:
            pprev = pprev_ref[...]
        hcat = jnp.concatenate([pprev, h], axis=0)
        pnew_ref[...] = hcat[rows:]
        if lay.nt > 1:
            pcarry_ref[...] = hcat[rows:]
        first = 2 * SUB
    for gi, win in enumerate(POOL_WINDOWS):
        cols = slice(gi * POOL_GW, (gi + 1) * POOL_GW)
        if lay.nseq:
            lo = first - (win - 1) * step
            wsum = _window_sum(hcat[:, cols], gi + 1, step)[lo:lo + rows]
        else:
            s = hcat[:, cols]
            for lv in range(gi + 1):
                s = s + pltpu.roll(s, 1 << lv, 0)
            wsum = s[first:]
        cnt = jnp.minimum(win, pos + 1).astype(F32)
        pooled = wsum / cnt - h[:, cols]
        mix_ref[:, cols] = _dot(pooled.astype(BF16), pw_ref[gi])
    x3 = x + mix_ref[...] * ps_ref[...]
    hf = _rms(x3, nf_ref[...]).astype(BF16)
    f = _conv_ffn(hf, lay, prev_ref, carry_ref, new_ref, wup_ref, cw_ref, cb_ref, wdown_ref, act_ref)
    lay.store(y_ref, _rms(x3 + f, nfin_ref[...]))


def _poolffn(x, pool_prev, prev, p, *, n, nb, nt, rows, lay, start_pos):
    row_spec = functools.partial(lay.row_spec, rows)
    return pl.pallas_call(
        functools.partial(_poolffn_kernel, lay=lay, start_pos=start_pos),
        grid=(nb, nt),
        in_specs=[row_spec(D_MODEL), lay.state_spec(pool_prev, nb),
                  _const_spec((1, D_MODEL)), _const_spec((4, POOL_GW, POOL_GW)), _const_spec((1, D_MODEL)),
                  _const_spec((1, D_MODEL))] + _ffn_specs(lay, prev, nb) + [_const_spec((1, D_MODEL))],
        out_specs=[row_spec(D_MODEL), lay.state_spec(pool_prev, nb), lay.state_spec(prev, nb)],
        out_shape=[lay.row_shape(n, D_MODEL), jax.ShapeDtypeStruct(pool_prev.shape, F32),
                   jax.ShapeDtypeStruct(prev.shape, F32)],
        scratch_shapes=[pltpu.VMEM((2 * SUB, D_MODEL), F32), pltpu.VMEM((SUB, 2 * D_FF), F32),
                        pltpu.VMEM((rows, D_FF), BF16), pltpu.VMEM((rows, D_MODEL), F32)],
        compiler_params=_PARAMS_2D,
        name="pool_ffn",
    )(x, pool_prev, p["norm_mix1"], p["pool_w"], p["pool_scale"], p["norm_ffn1"], prev, p["w_up1"],
      p["ffn_conv_w1"], p["ffn_conv_b1"], p["w_down1"], p["norm_final"])


def _trunk(x, n_seq, seq_len, conv_prev, s0, pool_prev, ffn_prev, start_pos, p):
    n = n_seq * seq_len
    if seq_len == SUB:
        rows = min(SHORT_ROW_TILE, n)
        geom = dict(n=n, nb=n // rows, nt=1, rows=rows, lay=_Layout(rows // SUB, 1))
    else:
        rows = min(ROW_TILE, seq_len)
        geom = dict(n=n, nb=n_seq, nt=seq_len // rows, rows=rows, lay=_Layout(0, seq_len // rows))
    tiled = seq_len % TILE_ROWS == 0
    q, k, v, z, gates, conv_new = _proj(x, conv_prev, p, BF16 if tiled else F32, **geom)
    if seq_len == SUB:
        flat = lambda a: a.reshape(n, a.shape[-1])
        o, s_new = _delta_short(flat(q), flat(k), flat(v), flat(gates), s0, nseq=1)
        o = o.reshape(n_seq, SUB, VAL)
    elif tiled:
        o, s_new = _delta_tiles(q, k, v, gates, s0, nb=n_seq, nt=seq_len // TILE_ROWS)
    else:
        o, s_new = _delta_long(q, k, v, gates, s0, chunk=min(GDN_CHUNK, seq_len), **geom)
    x2, ffn_new0 = _mixffn(x, o, z, ffn_prev[0], p, **geom)
    y, pool_new, ffn_new1 = _poolffn(x2, pool_prev, ffn_prev[1], p, start_pos=start_pos, **geom)
    return y, conv_new, s_new, pool_new, (ffn_new0, ffn_new1)


def _prep_params(norm_mix, norm_ffn, gdn_w_in, gdn_conv_w, gdn_A_log, gdn_dt_bias, gdn_norm_w, gdn_w_out,
                 pool_w, pool_scale, ffn_w_up, ffn_conv_w, ffn_conv_b, ffn_w_down, norm_final):
    w_in = gdn_w_in[0]
    gate_pad = ((0, 0), (0, LANE - 2 * HEADS))
    head_pad = ((0, 0), (HEADS, LANE - 2 * HEADS))
    return {
        "norm_mix0": norm_mix[0:1], "norm_mix1": norm_mix[1:2],
        "norm_ffn0": norm_ffn[0:1], "norm_ffn1": norm_ffn[1:2],
        "w_qkv": w_in[:, :QKV].astype(BF16),
        "w_z": w_in[:, QKV:QKV + VAL].astype(BF16),
        "w_g": jnp.pad(w_in[:, QKV + VAL:], gate_pad).astype(BF16),
        "gdn_conv_w": gdn_conv_w[0],
        "a_log": jnp.pad(gdn_A_log[0:1], head_pad),
        "dt_bias": jnp.pad(gdn_dt_bias[0:1], head_pad),
        "gdn_norm_w": gdn_norm_w[0:1],
        "w_out": gdn_w_out[0].astype(BF16),
        "pool_w": pool_w[0].astype(BF16),
        "pool_scale": pool_scale[0:1],
        "w_up0": ffn_w_up[0].astype(BF16), "w_up1": ffn_w_up[1].astype(BF16),
        "ffn_conv_w0": ffn_conv_w[0], "ffn_conv_w1": ffn_conv_w[1],
        "ffn_conv_b0": ffn_conv_b[0:1], "ffn_conv_b1": ffn_conv_b[1:2],
        "w_down0": ffn_w_down[0].astype(BF16), "w_down1": ffn_w_down[1].astype(BF16),
        "norm_final": norm_final[None],
    }


def kernel(x_prompt, x_sample, state_gdn_conv, state_gdn_rec, state_pool, state_ffn_conv, meta_tokens, norm_mix, norm_ffn, gdn_w_in, gdn_conv_w, gdn_A_log, gdn_dt_bias, gdn_norm_w, gdn_w_out, pool_w, pool_scale, ffn_w_up, ffn_conv_w, ffn_conv_b, ffn_w_down, norm_final):
    p = _prep_params(norm_mix, norm_ffn, gdn_w_in, gdn_conv_w, gdn_A_log, gdn_dt_bias, gdn_norm_w, gdn_w_out,
                     pool_w, pool_scale, ffn_w_up, ffn_conv_w, ffn_conv_b, ffn_w_down, norm_final)
    bp, lp, _ = x_prompt.shape
    bs, ls, _ = x_sample.shape

    _, m_conv, m_s, m_pool, m_ffn = _trunk(
        meta_tokens.astype(F32), 1, N_META,
        jnp.zeros((SUB, QKV), F32), jnp.zeros((1, HEADS, DK, DV), F32), jnp.zeros((2 * SUB, D_MODEL), F32),
        (jnp.zeros((SUB, 2 * D_FF), F32), jnp.zeros((SUB, 2 * D_FF), F32)), 0, p)

    rep = lambda a: jnp.tile(a, (bp, 1))
    yp, p_conv, p_s, p_pool, p_ffn = _trunk(
        x_prompt.reshape(bp * lp, D_MODEL), bp, lp,
        rep(m_conv), jnp.broadcast_to(m_s, (bp, HEADS, DK, DV)), rep(m_pool), (rep(m_ffn[0]), rep(m_ffn[1])),
        N_META, p)

    tm = lambda a: jnp.swapaxes(a, 0, 1)
    ys, s_conv, s_s, s_pool, s_ffn = _trunk(
        x_sample, bs, ls,
        tm(state_gdn_conv[0]), state_gdn_rec[0], tm(state_pool[0]),
        (tm(state_ffn_conv[0]), tm(state_ffn_conv[1])), PAST_LEN, p)

    last = lambda a, b, r: a.reshape(b, -1, a.shape[-1])[:, -r:]
    return (yp.reshape(bp, lp, D_MODEL), ys,
            last(p_conv, bp, CONV_W - 1)[None], p_s[None], last(p_pool, bp, POOL_BUF)[None],
            jnp.stack([last(p_ffn[0], bp, FFN_CONV_W - 1), last(p_ffn[1], bp, FFN_CONV_W - 1)]),
            tm(s_conv)[None], s_s[None], tm(s_pool)[None],
            jnp.stack([tm(s_ffn[0]), tm(s_ffn[1])]))
```

```python
import functools

import jax
import jax.numpy as jnp
from jax import lax
from jax.experimental import pallas as pl
from jax.experimental.pallas import tpu as pltpu

D_MODEL = 1024
N_META = 16
EPS = 1e-6
HEADS = 8
DK = 128
DV = 128
KEY = HEADS * DK
VAL = HEADS * DV
QKV = 2 * KEY + VAL
CONV_W = 4
GDN_CHUNK = 64
POOL_WINDOWS = (2, 4, 8, 16)
POOL_GW = D_MODEL // 4
POOL_BUF = 15
D_FF = 2816
FFN_CONV_W = 3
PAST_LEN = 16384

SUB = 8
LANE = 128
FFN_COLS = 256
ROW_TILE = 512
SHORT_ROW_TILE = 256
SHORT_SEQS = 8
VMEM_LIMIT = 56 * 1024 * 1024

F32 = jnp.float32
BF16 = jnp.bfloat16
HI = lax.Precision.HIGHEST


def _dot(a, b):
    return jnp.dot(a.astype(BF16), b.astype(BF16), preferred_element_type=F32)


def _dot_nt(a, b):
    return lax.dot_general(a.astype(BF16), b.astype(BF16), (((1,), (1,)), ((), ())), preferred_element_type=F32)


def _dot_tn(a, b):
    return lax.dot_general(a.astype(BF16), b.astype(BF16), (((0,), (0,)), ((), ())), preferred_element_type=F32)


def _rms(x, w):
    return x * lax.rsqrt(jnp.mean(x * x, axis=-1, keepdims=True) + EPS) * w


def _silu(x):
    return x * jax.nn.sigmoid(x)


def _softplus(x):
    return jnp.maximum(x, 0.0) + jnp.log1p(jnp.exp(-jnp.abs(x)))


def _conv_taps(u, prev, width, nseq):
    rows = u.shape[0]
    xcat = jnp.concatenate([prev, u], axis=0)
    if nseq:
        return [xcat[j * nseq:j * nseq + rows] for j in range(width)], xcat[rows:]
    taps = [pltpu.roll(xcat, width - 1 - j, 0)[SUB:] for j in range(width - 1)] + [u]
    return taps, u[rows - SUB:]


class _Layout:
    def __init__(self, nseq, nt):
        self.nseq = nseq
        self.nt = nt

    def load(self, ref, cols=slice(None)):
        if self.nseq:
            return jnp.concatenate([ref[:, t, cols] for t in range(SUB)], axis=0)
        return ref[:, cols]

    def store(self, ref, val, cols=slice(None)):
        val = val.astype(ref.dtype)
        if self.nseq:
            for t in range(SUB):
                ref[:, t, cols] = val[t * self.nseq:(t + 1) * self.nseq]
        else:
            ref[:, cols] = val

    def load_state(self, ref, cols=slice(None)):
        return jnp.concatenate([ref[i, :, cols] for i in range(ref.shape[0])], axis=0)

    def store_state(self, ref, val, cols=slice(None)):
        for i in range(ref.shape[0]):
            ref[i, :, cols] = val[i * self.nseq:(i + 1) * self.nseq]

    def init_carry(self, carry_ref, prev_ref):
        if not self.nseq and self.nt > 1:
            @pl.when(pl.program_id(1) == 0)
            def _():
                carry_ref[...] = prev_ref[...]

    def conv(self, u, w_ref, prev_ref, carry_ref, new_ref, cols, width):
        if self.nseq:
            prev = self.load_state(prev_ref, cols)
        else:
            prev = (carry_ref if self.nt > 1 else prev_ref)[:, cols]
        taps, new = _conv_taps(u, prev, width, self.nseq)
        out = w_ref[0:1, cols] * taps[0]
        for j in range(1, width):
            out = out + w_ref[j:j + 1, cols] * taps[j]
        if self.nseq:
            self.store_state(new_ref, new, cols)
        else:
            new_ref[:, cols] = new
            if self.nt > 1:
                carry_ref[:, cols] = new
        return out

    def row_spec(self, rows, c):
        if self.nseq:
            return pl.BlockSpec((self.nseq, SUB, c), lambda b, t: (b, 0, 0))
        nt = self.nt
        return pl.BlockSpec((rows, c), lambda b, t: (b * nt + t, 0))

    def row_shape(self, n, c, dtype=F32):
        return jax.ShapeDtypeStruct((n // SUB, SUB, c) if self.nseq else (n, c), dtype)

    def state_spec(self, arr, nb):
        if self.nseq:
            return pl.BlockSpec((arr.shape[0], self.nseq, arr.shape[2]), lambda b, t: (0, b, 0))
        return pl.BlockSpec((arr.shape[0] // nb, arr.shape[1]), lambda b, t: (b, 0))


def _proj_kernel(x_ref, prev_ref, nw_ref, wqkv_ref, wz_ref, wg_ref, cw_ref, alog_ref, dtb_ref,
                 q_ref, k_ref, v_ref, z_ref, g_ref, cnew_ref, carry_ref, *, lay):
    h = _rms(lay.load(x_ref), nw_ref[...]).astype(BF16)
    lay.store(z_ref, _dot(h, wz_ref[...]))
    lg = _dot(h, wg_ref[...])
    beta = jax.nn.sigmoid(lg)
    g = -jnp.exp(alog_ref[...]) * _softplus(lg + dtb_ref[...])
    lane = lax.broadcasted_iota(jnp.int32, lg.shape, 1)
    lay.store(g_ref, jnp.where(lane < HEADS, beta, g))

    lay.init_carry(carry_ref, prev_ref)
    for part, out_ref in enumerate((q_ref, k_ref, v_ref)):
        cols = slice(part * KEY, (part + 1) * KEY)
        raw = _dot(h, wqkv_ref[:, cols])
        act = _silu(lay.conv(raw, cw_ref, prev_ref, carry_ref, cnew_ref, cols, CONV_W))
        if part == 2:
            lay.store(out_ref, act)
            continue
        for hd in range(HEADS):
            hs = slice(hd * DK, (hd + 1) * DK)
            a = act[:, hs]
            n = a * lax.rsqrt(jnp.sum(a * a, axis=-1, keepdims=True) + EPS)
            lay.store(out_ref, n * (DK ** -0.5) if part == 0 else n, hs)


def _const_spec(shape):
    nd = len(shape)
    return pl.BlockSpec(shape, lambda b, t: (0,) * nd, pipeline_mode=pl.Buffered(1))


_PARAMS_2D = pltpu.CompilerParams(dimension_semantics=("arbitrary", "arbitrary"), vmem_limit_bytes=VMEM_LIMIT)


def _proj(x, prev, p, qkv_dtype, *, n, nb, nt, rows, lay):
    row_spec = functools.partial(lay.row_spec, rows)
    row_shape = functools.partial(lay.row_shape, n)
    qkv_shape = functools.partial(lay.row_shape, n, dtype=qkv_dtype)
    return pl.pallas_call(
        functools.partial(_proj_kernel, lay=lay),
        grid=(nb, nt),
        in_specs=[row_spec(D_MODEL), lay.state_spec(prev, nb), _const_spec((1, D_MODEL)),
                  _const_spec((D_MODEL, QKV)), _const_spec((D_MODEL, VAL)), _const_spec((D_MODEL, LANE)),
                  _const_spec((CONV_W, QKV)), _const_spec((1, LANE)), _const_spec((1, LANE))],
        out_specs=[row_spec(KEY), row_spec(KEY), row_spec(VAL), row_spec(VAL), row_spec(LANE),
                   lay.state_spec(prev, nb)],
        out_shape=[qkv_shape(KEY), qkv_shape(KEY), qkv_shape(VAL), row_shape(VAL), row_shape(LANE),
                   jax.ShapeDtypeStruct(prev.shape, F32)],
        scratch_shapes=[pltpu.VMEM((SUB, QKV), F32)],
        compiler_params=_PARAMS_2D,
        name="gdn_proj",
    )(x, prev, p["norm_mix0"], p["w_qkv"], p["w_z"], p["w_g"], p["gdn_conv_w"], p["a_log"], p["dt_bias"])


def _chunk_terms(units, same, n_sq):
    c = units[0][0].shape[0]
    ri = lax.broadcasted_iota(jnp.int32, (c, c), 0)
    ci = lax.broadcasted_iota(jnp.int32, (c, c), 1)
    eye_b = ri == ci
    eye = eye_b.astype(F32)
    causal = same & (ri >= ci)
    pre = []
    for q, k, v, beta_b, g_b in units:
        g_row = jnp.sum(jnp.where(eye_b, g_b, 0.0), axis=0, keepdims=True)
        cum_row = jnp.sum(jnp.where(same & (ri <= ci), g_b, 0.0), axis=0, keepdims=True)
        cum_col = jnp.sum(jnp.where(eye_b, cum_row, 0.0), axis=1, keepdims=True)
        g_tot = jnp.sum(jnp.where(same, g_row, 0.0), axis=1, keepdims=True)
        decay = jnp.exp(jnp.where(causal, cum_col - cum_row, -jnp.inf))
        pre.append((decay, jnp.exp(cum_col), jnp.exp(g_tot - cum_col), g_tot))
    kk = [_dot_nt(k, k) for _, k, _, _, _ in units]
    qk = [_dot_nt(q, k) for q, k, _, _, _ in units]
    low = [jnp.where(same & (ri > ci), un[3] * kk_u * pr[0], 0.0) for un, kk_u, pr in zip(units, kk, pre)]
    x = [eye - lw for lw in low]
    pw = low
    for _ in range(n_sq):
        pw = [jnp.dot(m, m, precision=HI, preferred_element_type=F32) for m in pw]
        x = [xi + jnp.dot(xi, m, precision=HI, preferred_element_type=F32) for xi, m in zip(x, pw)]
    u = [_dot(t, un[2] * un[3][:, 0:1]) for t, un in zip(x, units)]
    w = [_dot(t, un[1] * (un[3][:, 0:1] * pr[1])) for t, un, pr in zip(x, units, pre)]
    return [(u_u, w_u, qk_u * pr[0], un[0] * pr[1], un[1] * pr[2], pr[3])
            for u_u, w_u, qk_u, un, pr in zip(u, w, qk, units, pre)]


def _delta_long_kernel(q_ref, k_ref, v_ref, g_ref, s0_ref, o_ref, s_ref, *, chunk, n_chunks, n_sq):
    t = pl.program_id(1)

    @pl.when(t == 0)
    def _():
        s_ref[...] = s0_ref[...]

    same = jnp.full((chunk, chunk), True)

    def body(c, carry):
        rows = pl.ds(pl.multiple_of(c * chunk, chunk), chunk)
        gates = g_ref[rows, :]
        heads = range(HEADS)
        hcols = [slice(hd * DK, (hd + 1) * DK) for hd in heads]
        terms = _chunk_terms(
            [(q_ref[rows, hs], k_ref[rows, hs], v_ref[rows, hs],
              jnp.broadcast_to(gates[:, hd:hd + 1], (chunk, chunk)),
              jnp.broadcast_to(gates[:, HEADS + hd:HEADS + hd + 1], (chunk, chunk)))
             for hd, hs in zip(heads, hcols)], same, n_sq)
        s = [s_ref[0, hd] for hd in heads]
        v_new = [tm[0] - _dot(tm[1], s_h) for tm, s_h in zip(terms, s)]
        qs = [_dot(tm[3], s_h) for tm, s_h in zip(terms, s)]
        av = [_dot(tm[2], vn) for tm, vn in zip(terms, v_new)]
        kv = [_dot_tn(tm[4], vn) for tm, vn in zip(terms, v_new)]
        for hd in heads:
            o_ref[rows, hcols[hd]] = qs[hd] + av[hd]
            s_ref[0, hd] = s[hd] * jnp.exp(terms[hd][5][0:1, :]) + kv[hd]
        return carry

    lax.fori_loop(0, n_chunks, body, 0)


def _delta_long(q, k, v, gates, s0, *, n, nb, nt, rows, lay, chunk):
    row_spec = functools.partial(lay.row_spec, rows)
    s_spec = pl.BlockSpec((1, HEADS, DK, DV), lambda b, t: (b, 0, 0, 0))
    n_sq = chunk.bit_length() - 2
    kern = functools.partial(_delta_long_kernel, chunk=chunk, n_chunks=rows // chunk, n_sq=n_sq)
    return pl.pallas_call(
        kern,
        grid=(nb, nt),
        in_specs=[row_spec(KEY), row_spec(KEY), row_spec(VAL), row_spec(LANE), s_spec],
        out_specs=[row_spec(VAL), s_spec],
        out_shape=[jax.ShapeDtypeStruct((n, VAL), F32), jax.ShapeDtypeStruct(s0.shape, F32)],
        compiler_params=_PARAMS_2D,
        name="gdn_delta_long",
    )(q, k, v, gates, s0)


UNITS = LANE
TILE_CHUNKS = UNITS // HEADS
TILE_ROWS = TILE_CHUNKS * GDN_CHUNK


def _col(x, lane, width):
    return jnp.broadcast_to(x[:, lane:lane + 1], (x.shape[0], width))


def _delta_tile_kernel(q_ref, k_ref, v_ref, g_ref, s0_ref, o_ref, s_ref,
                       low_scr, lowt_scr, tcol_scr, tinv_scr, u_scr, kg_scr, w_scr, qd_scr, kd_scr, a_scr, gt_scr,
                       beta_scr):
    @pl.when(pl.program_id(1) == 0)
    def _():
        s_ref[...] = s0_ref[...]

    chunk = GDN_CHUNK
    half = chunk // 2
    row_id = lax.broadcasted_iota(jnp.int32, (chunk, LANE), 0)
    lane_id = lax.broadcasted_iota(jnp.int32, (chunk, LANE), 1)
    col_id = lane_id % chunk
    first_half = lax.broadcasted_iota(jnp.int32, (half, LANE), 1) < chunk
    heads = range(HEADS)
    hcols = [slice(h * DK, (h + 1) * DK) for h in heads]

    def phase_1a(c, carry):
        rows = pl.ds(pl.multiple_of(c * chunk, chunk), chunk)
        gates = g_ref[rows, :]
        cum = gates
        d = 1
        while d < chunk:
            cum = cum + jnp.where(row_id >= d, pltpu.roll(cum, d, 0), 0.0)
            d *= 2
        tot = cum[chunk - 1:chunk, :]
        cum_t = jnp.concatenate([cum, cum], axis=0).T
        beta_scr[...] = jnp.where(lane_id // HEADS == c, pltpu.roll(gates, c * HEADS, 1), beta_scr[...])
        def scores(h):
            qh, kh = q_ref[rows, hcols[h]], k_ref[rows, hcols[h]]
            return qh, kh, _dot_nt(jnp.concatenate([qh, kh], axis=0), jnp.concatenate([kh, kh], axis=0))

        ahead = scores(0)
        for h in heads:
            qh, kh, sc = ahead
            if h + 1 < HEADS:
                ahead = scores(h + 1)
            cum_b = _col(cum, HEADS + h, DK)
            tot_b = jnp.broadcast_to(tot[:, HEADS + h:HEADS + h + 1], (chunk, DK))
            diff = cum_b - jnp.broadcast_to(cum_t[HEADS + h:HEADS + h + 1, :], (chunk, LANE))
            decay = jnp.exp(jnp.where(row_id >= col_id, diff, -jnp.inf))
            unit = c * HEADS + h
            low = jnp.where(row_id > col_id, sc[chunk:] * decay, 0.0)
            low_scr[pl.ds(unit, half, stride=UNITS), :] = jnp.where(first_half, low[:half], low[half:])
            a_scr[unit] = (sc[:chunk] * decay)[:, :chunk].astype(BF16)
            e1 = jnp.exp(cum_b)
            kf = kh.astype(F32)
            kg_scr[rows, hcols[h]] = (kf * e1).astype(BF16)
            qd_scr[rows, hcols[h]] = (qh.astype(F32) * e1).astype(BF16)
            kd_scr[rows, hcols[h]] = (kf * jnp.exp(tot_b - cum_b)).astype(BF16)
        e_tot = jnp.exp(tot)
        gt_scr[pl.ds(pl.multiple_of(c * HEADS, HEADS), HEADS), :] = jnp.concatenate(
            [jnp.broadcast_to(e_tot[:, HEADS + h:HEADS + h + 1], (1, LANE)) for h in heads], axis=0)
        return carry

    beta_scr[...] = jnp.zeros(beta_scr.shape, F32)
    lax.fori_loop(0, TILE_CHUNKS, phase_1a, 0)

    for i2 in range(half):
        slab = low_scr[pl.ds(i2 * UNITS, UNITS), :].T
        lowt_scr[i2] = slab[:chunk]
        lowt_scr[i2 + half] = slab[chunk:]
    sub = lax.broadcasted_iota(jnp.int32, (SUB, LANE), 0)
    zero = jnp.zeros((SUB, LANE), F32)
    for i in range(chunk):
        nblk = i // SUB + 1
        acc = [[zero, zero] for _ in range(nblk)]
        if i:
            beta_i = jnp.broadcast_to(beta_scr[i:i + 1, :], (SUB, LANE))
        for j in range(i):
            lrow = jnp.broadcast_to(lowt_scr[i, j:j + 1, :], (SUB, LANE))
            for cb in range(j // SUB + 1):
                acc[cb][j % 2] = acc[cb][j % 2] + lrow * tcol_scr[j, cb * SUB:(cb + 1) * SUB, :]
        for cb in range(chunk // SUB):
            blk = zero
            if i and cb < nblk:
                blk = -(beta_i * (acc[cb][0] + acc[cb][1]))
            if cb == i // SUB:
                blk = blk + (sub == i % SUB).astype(F32)
            tcol_scr[i, cb * SUB:(cb + 1) * SUB, :] = blk
    for i2 in range(half):
        tinv_scr[pl.ds(i2 * UNITS, UNITS), :] = jnp.concatenate([tcol_scr[i2], tcol_scr[i2 + half]], axis=0).T

    def uw_compute(c):
        rows = pl.ds(pl.multiple_of(c * chunk, chunk), chunk)
        gates = g_ref[rows, :]
        gates_t = jnp.concatenate([gates, gates], axis=0).T
        out = []
        lhs = []
        for h in heads:
            packed = tinv_scr[pl.ds(c * HEADS + h, half, stride=UNITS), :]
            packed = packed * jnp.broadcast_to(gates_t[h:h + 1, :], (half, LANE))
            lhs.append(jnp.concatenate([jnp.where(first_half, packed, 0.0),
                                        jnp.where(first_half, 0.0, packed)], axis=0))
        for h in heads:
            vk = jnp.concatenate([v_ref[rows, hcols[h]], kg_scr[rows, hcols[h]]], axis=1)
            out.append(_dot(lhs[h], jnp.concatenate([vk, vk], axis=0)))
        return out

    def uw_store(c, uw):
        rows = pl.ds(pl.multiple_of(c * chunk, chunk), chunk)
        for h in heads:
            u_scr[rows, hcols[h]] = uw[h][:, :DV]
            w_scr[rows, hcols[h]] = uw[h][:, DV:].astype(BF16)

    uw_store(0, uw_compute(0))

    def phase_2(c, carry):
        nxt = jnp.minimum(c + 1, TILE_CHUNKS - 1)
        rows = pl.ds(pl.multiple_of(c * chunk, chunk), chunk)
        s = [s_ref[0, h] for h in heads]
        m1 = [_dot(jnp.concatenate([w_scr[rows, hcols[h]], qd_scr[rows, hcols[h]]], axis=0), s[h].astype(BF16))
              for h in heads]
        uw = uw_compute(nxt)
        vb = [(u_scr[rows, hcols[h]] - m1[h][:chunk]).astype(BF16) for h in heads]
        av = [_dot(a_scr[c * HEADS + h], vb[h]) for h in heads]
        kv = [_dot_tn(kd_scr[rows, hcols[h]], vb[h]) for h in heads]
        for h in heads:
            o_ref[rows, hcols[h]] = m1[h][chunk:] + av[h]
            s_ref[0, h] = s[h] * gt_scr[pl.ds(c * HEADS + h, 1), :] + kv[h]
        uw_store(nxt, uw)
        return carry

    lax.fori_loop(0, TILE_CHUNKS, phase_2, 0)


def _delta_tiles(q, k, v, gates, s0, *, nb, nt):
    n = q.shape[0]
    row_spec = lambda c: pl.BlockSpec((TILE_ROWS, c), lambda b, t: (b * nt + t, 0))
    s_spec = pl.BlockSpec((1, HEADS, DK, DV), lambda b, t: (b, 0, 0, 0))
    packed = pltpu.VMEM((GDN_CHUNK // 2 * UNITS, LANE), F32)
    by_row = pltpu.VMEM((GDN_CHUNK, GDN_CHUNK, UNITS), F32)
    tile_bf16 = pltpu.VMEM((TILE_ROWS, KEY), BF16)
    return pl.pallas_call(
        _delta_tile_kernel,
        grid=(nb, nt),
        in_specs=[row_spec(KEY), row_spec(KEY), row_spec(VAL), row_spec(LANE), s_spec],
        out_specs=[row_spec(VAL), s_spec],
        out_shape=[jax.ShapeDtypeStruct((n, VAL), F32), jax.ShapeDtypeStruct(s0.shape, F32)],
        scratch_shapes=[packed, by_row, by_row, packed,
                        pltpu.VMEM((TILE_ROWS, VAL), F32), tile_bf16, tile_bf16, tile_bf16, tile_bf16,
                        pltpu.VMEM((UNITS, GDN_CHUNK, GDN_CHUNK), BF16), pltpu.VMEM((UNITS, LANE), F32),
                        pltpu.VMEM((GDN_CHUNK, UNITS), F32)],
        compiler_params=_PARAMS_2D,
        name="gdn_delta_tiles",
    )(q, k, v, gates, s0)


def _delta_short_kernel(q_ref, k_ref, v_ref, g_ref, s0_ref, o_ref, s_ref, *, nseq):
    c = HEADS * SUB
    ri = lax.broadcasted_iota(jnp.int32, (c, c), 0)
    ci = lax.broadcasted_iota(jnp.int32, (c, c), 1)
    same = (ri // SUB) == (ci // SUB)
    rh = lax.broadcasted_iota(jnp.int32, (2 * c, KEY), 0) % c // SUB
    ch = lax.broadcasted_iota(jnp.int32, (2 * c, KEY), 1) // DK
    head_cols2 = rh == ch
    head_cols = head_cols2[:c]
    heads = range(HEADS)
    seqs = range(nseq)
    rows = [slice(sq * SUB, (sq + 1) * SUB) for sq in seqs]

    def unit(sq):
        gates = g_ref[rows[sq], :]
        heads_to_rows = lambda ref: jnp.concatenate(
            [ref[rows[sq], hd * DK:(hd + 1) * DK] for hd in heads], axis=0)
        gate_rows = lambda lane0: jnp.concatenate(
            [jnp.broadcast_to(gates[:, lane0 + hd:lane0 + hd + 1], (SUB, c)) for hd in heads], axis=0)
        return heads_to_rows(q_ref), heads_to_rows(k_ref), heads_to_rows(v_ref), gate_rows(0), gate_rows(HEADS)

    terms = _chunk_terms([unit(sq) for sq in seqs], same, 2)
    spread = lambda m: jnp.where(head_cols if m.shape[0] == c else head_cols2,
                                 jnp.concatenate([m] * HEADS, axis=1), 0.0)
    s = [s0_ref[sq].reshape(KEY, DV) for sq in seqs]
    ws = [_dot(spread(jnp.concatenate([tm[1], tm[3]], axis=0)), s_q) for tm, s_q in zip(terms, s)]
    v_new = [tm[0] - m[:c] for tm, m in zip(terms, ws)]
    qs = [m[c:] for m in ws]
    av = [_dot(tm[2], vn) for tm, vn in zip(terms, v_new)]
    upd = [_dot_tn(spread(tm[4]), vn) for tm, vn in zip(terms, v_new)]
    for sq in seqs:
        o = qs[sq] + av[sq]
        o_ref[rows[sq], :] = jnp.concatenate([o[hd * SUB:(hd + 1) * SUB] for hd in heads], axis=1)
        for hd in heads:
            ks = slice(hd * DK, (hd + 1) * DK)
            s_ref[sq, hd] = s[sq][ks] * jnp.exp(terms[sq][5][hd * SUB:hd * SUB + 1, :]) + upd[sq][ks]


def _delta_short(q, k, v, gates, s0, *, nseq):
    n = q.shape[0]
    nb = n // (nseq * SUB)
    row_spec = lambda c: pl.BlockSpec((nseq * SUB, c), lambda b: (b, 0))
    s_spec = pl.BlockSpec((nseq, HEADS, DK, DV), lambda b: (b, 0, 0, 0))
    return pl.pallas_call(
        functools.partial(_delta_short_kernel, nseq=nseq),
        grid=(nb,),
        in_specs=[row_spec(KEY), row_spec(KEY), row_spec(VAL), row_spec(LANE), s_spec],
        out_specs=[row_spec(VAL), s_spec],
        out_shape=[jax.ShapeDtypeStruct((n, VAL), F32), jax.ShapeDtypeStruct(s0.shape, F32)],
        compiler_params=pltpu.CompilerParams(dimension_semantics=("arbitrary",),
                                             vmem_limit_bytes=VMEM_LIMIT),
        name="gdn_delta_short",
    )(q, k, v, gates, s0)


def _conv_ffn(h, lay, prev_ref, carry_ref, new_ref, wup_ref, cw_ref, cb_ref, wdown_ref, act_ref):
    lay.init_carry(carry_ref, prev_ref)
    for c in range(D_FF // FFN_COLS):
        halves = []
        for part in range(2):
            lo = part * D_FF + c * FFN_COLS
            cols = slice(lo, lo + FFN_COLS)
            u = _dot(h, wup_ref[:, cols])
            halves.append(lay.conv(u, cw_ref, prev_ref, carry_ref, new_ref, cols, FFN_CONV_W) + cb_ref[:, cols])
        act_ref[:, c * FFN_COLS:(c + 1) * FFN_COLS] = (_silu(halves[0]) * halves[1]).astype(BF16)
    return _dot(act_ref[...], wdown_ref[...])


def _mixffn_kernel(x_ref, o_ref, z_ref, gnw_ref, wout_ref, nf_ref, prev_ref, wup_ref, cw_ref, cb_ref, wdown_ref,
                   y_ref, new_ref, carry_ref, act_ref, gated_ref, *, lay):
    for hd in range(HEADS):
        hs = slice(hd * DV, (hd + 1) * DV)
        o = lay.load(o_ref, hs)
        o = o * lax.rsqrt(jnp.mean(o * o, axis=-1, keepdims=True) + EPS) * gnw_ref[...]
        gated_ref[:, hs] = (o * _silu(lay.load(z_ref, hs))).astype(BF16)
    x1 = lay.load(x_ref) + _dot(gated_ref[...], wout_ref[...])
    h = _rms(x1, nf_ref[...]).astype(BF16)
    f = _conv_ffn(h, lay, prev_ref, carry_ref, new_ref, wup_ref, cw_ref, cb_ref, wdown_ref, act_ref)
    lay.store(y_ref, x1 + f)


def _ffn_specs(lay, prev, nb):
    return [lay.state_spec(prev, nb), _const_spec((D_MODEL, 2 * D_FF)),
            _const_spec((FFN_CONV_W, 2 * D_FF)), _const_spec((1, 2 * D_FF)), _const_spec((D_FF, D_MODEL))]


def _mixffn(x, o, z, prev, p, *, n, nb, nt, rows, lay):
    row_spec = functools.partial(lay.row_spec, rows)
    return pl.pallas_call(
        functools.partial(_mixffn_kernel, lay=lay),
        grid=(nb, nt),
        in_specs=[row_spec(D_MODEL), row_spec(VAL), row_spec(VAL), _const_spec((1, DV)),
                  _const_spec((VAL, D_MODEL)), _const_spec((1, D_MODEL))] + _ffn_specs(lay, prev, nb),
        out_specs=[row_spec(D_MODEL), lay.state_spec(prev, nb)],
        out_shape=[lay.row_shape(n, D_MODEL), jax.ShapeDtypeStruct(prev.shape, F32)],
        scratch_shapes=[pltpu.VMEM((SUB, 2 * D_FF), F32), pltpu.VMEM((rows, D_FF), BF16),
                        pltpu.VMEM((rows, VAL), BF16)],
        compiler_params=_PARAMS_2D,
        name="gdn_out_ffn",
    )(x, o, z, p["gdn_norm_w"], p["w_out"], p["norm_ffn0"], prev, p["w_up0"], p["ffn_conv_w0"],
      p["ffn_conv_b0"], p["w_down0"])


def _window_sum(hcat, level, step):
    s = hcat
    for lv in range(level):
        d = step << lv
        s = s[d:] + s[:s.shape[0] - d]
    return s


def _poolffn_kernel(x_ref, pprev_ref, nm_ref, pw_ref, ps_ref, nf_ref, prev_ref, wup_ref, cw_ref, cb_ref, wdown_ref,
                    nfin_ref, y_ref, pnew_ref, new_ref, pcarry_ref, carry_ref, act_ref, mix_ref, *, lay, start_pos):
    x = lay.load(x_ref)
    rows = x.shape[0]
    h = _rms(x, nm_ref[...])
    row = lax.broadcasted_iota(jnp.int32, (rows, 1), 0)
    if lay.nseq:
        step = lay.nseq
        pos = start_pos + row // step
        hcat = jnp.concatenate([lay.load_state(pprev_ref), h], axis=0)
        lay.store_state(pnew_ref, hcat[rows:])
        first = POOL_BUF * step
    else:
        step = 1
        pos = start_pos + pl.program_id(1) * rows + row
        if lay.nt > 1:
            @pl.when(pl.program_id(1) == 0)
            def _():
                pcarry_ref[...] = pprev_ref[...]
            pprev = pcarry_ref[...]
        else:
            pprev = pprev_ref[...]
        hcat = jnp.concatenate([pprev, h], axis=0)
        pnew_ref[...] = hcat[rows:]
        if lay.nt > 1:
            pcarry_ref[...] = hcat[rows:]
        first = 2 * SUB
    for gi, win in enumerate(POOL_WINDOWS):
        cols = slice(gi * POOL_GW, (gi + 1) * POOL_GW)
        if lay.nseq:
            lo = first - (win - 1) * step
            wsum = _window_sum(hcat[:, cols], gi + 1, step)[lo:lo + rows]
        else:
            s = hcat[:, cols]
            for lv in range(gi + 1):
                s = s + pltpu.roll(s, 1 << lv, 0)
            wsum = s[first:]
        cnt = jnp.minimum(win, pos + 1).astype(F32)
        pooled = wsum / cnt - h[:, cols]
        mix_ref[:, cols] = _dot(pooled.astype(BF16), pw_ref[gi])
    x3 = x + mix_ref[...] * ps_ref[...]
    hf = _rms(x3, nf_ref[...]).astype(BF16)
    f = _conv_ffn(hf, lay, prev_ref, carry_ref, new_ref, wup_ref, cw_ref, cb_ref, wdown_ref, act_ref)
    lay.store(y_ref, _rms(x3 + f, nfin_ref[...]))


def _poolffn(x, pool_prev, prev, p, *, n, nb, nt, rows, lay, start_pos):
    row_spec = functools.partial(lay.row_spec, rows)
    return pl.pallas_call(
        functools.partial(_poolffn_kernel, lay=lay, start_pos=start_pos),
        grid=(nb, nt),
        in_specs=[row_spec(D_MODEL), lay.state_spec(pool_prev, nb),
                  _const_spec((1, D_MODEL)), _const_spec((4, POOL_GW, POOL_GW)), _const_spec((1, D_MODEL)),
                  _const_spec((1, D_MODEL))] + _ffn_specs(lay, prev, nb) + [_const_spec((1, D_MODEL))],
        out_specs=[row_spec(D_MODEL), lay.state_spec(pool_prev, nb), lay.state_spec(prev, nb)],
        out_shape=[lay.row_shape(n, D_MODEL), jax.ShapeDtypeStruct(pool_prev.shape, F32),
                   jax.ShapeDtypeStruct(prev.shape, F32)],
        scratch_shapes=[pltpu.VMEM((2 * SUB, D_MODEL), F32), pltpu.VMEM((SUB, 2 * D_FF), F32),
                        pltpu.VMEM((rows, D_FF), BF16), pltpu.VMEM((rows, D_MODEL), F32)],
        compiler_params=_PARAMS_2D,
        name="pool_ffn",
    )(x, pool_prev, p["norm_mix1"], p["pool_w"], p["pool_scale"], p["norm_ffn1"], prev, p["w_up1"],
      p["ffn_conv_w1"], p["ffn_conv_b1"], p["w_down1"], p["norm_final"])


def _trunk(x, n_seq, seq_len, conv_prev, s0, pool_prev, ffn_prev, start_pos, p):
    n = n_seq * seq_len
    if seq_len == SUB:
        rows = min(SHORT_ROW_TILE, n)
        geom = dict(n=n, nb=n // rows, nt=1, rows=rows, lay=_Layout(rows // SUB, 1))
    else:
        rows = min(ROW_TILE, seq_len)
        geom = dict(n=n, nb=n_seq, nt=seq_len // rows, rows=rows, lay=_Layout(0, seq_len // rows))
    tiled = seq_len % TILE_ROWS == 0
    q, k, v, z, gates, conv_new = _proj(x, conv_prev, p, BF16 if tiled else F32, **geom)
    if seq_len == SUB:
        flat = lambda a: a.reshape(n, a.shape[-1])
        o, s_new = _delta_short(flat(q), flat(k), flat(v), flat(gates), s0, nseq=min(SHORT_SEQS, n_seq))
        o = o.reshape(n_seq, SUB, VAL)
    elif tiled:
        o, s_new = _delta_tiles(q, k, v, gates, s0, nb=n_seq, nt=seq_len // TILE_ROWS)
    else:
        o, s_new = _delta_long(q, k, v, gates, s0, chunk=min(GDN_CHUNK, seq_len), **geom)
    x2, ffn_new0 = _mixffn(x, o, z, ffn_prev[0], p, **geom)
    y, pool_new, ffn_new1 = _poolffn(x2, pool_prev, ffn_prev[1], p, start_pos=start_pos, **geom)
    return y, conv_new, s_new, pool_new, (ffn_new0, ffn_new1)


def _prep_params(norm_mix, norm_ffn, gdn_w_in, gdn_conv_w, gdn_A_log, gdn_dt_bias, gdn_norm_w, gdn_w_out,
                 pool_w, pool_scale, ffn_w_up, ffn_conv_w, ffn_conv_b, ffn_w_down, norm_final):
    w_in = gdn_w_in[0]
    gate_pad = ((0, 0), (0, LANE - 2 * HEADS))
    head_pad = ((0, 0), (HEADS, LANE - 2 * HEADS))
    return {
        "norm_mix0": norm_mix[0:1], "norm_mix1": norm_mix[1:2],
        "norm_ffn0": norm_ffn[0:1], "norm_ffn1": norm_ffn[1:2],
        "w_qkv": w_in[:, :QKV].astype(BF16),
        "w_z": w_in[:, QKV:QKV + VAL].astype(BF16),
        "w_g": jnp.pad(w_in[:, QKV + VAL:], gate_pad).astype(BF16),
        "gdn_conv_w": gdn_conv_w[0],
        "a_log": jnp.pad(gdn_A_log[0:1], head_pad),
        "dt_bias": jnp.pad(gdn_dt_bias[0:1], head_pad),
        "gdn_norm_w": gdn_norm_w[0:1],
        "w_out": gdn_w_out[0].astype(BF16),
        "pool_w": pool_w[0].astype(BF16),
        "pool_scale": pool_scale[0:1],
        "w_up0": ffn_w_up[0].astype(BF16), "w_up1": ffn_w_up[1].astype(BF16),
        "ffn_conv_w0": ffn_conv_w[0], "ffn_conv_w1": ffn_conv_w[1],
        "ffn_conv_b0": ffn_conv_b[0:1], "ffn_conv_b1": ffn_conv_b[1:2],
        "w_down0": ffn_w_down[0].astype(BF16), "w_down1": ffn_w_down[1].astype(BF16),
        "norm_final": norm_final[None],
    }


def kernel(x_prompt, x_sample, state_gdn_conv, state_gdn_rec, state_pool, state_ffn_conv, meta_tokens, norm_mix, norm_ffn, gdn_w_in, gdn_conv_w, gdn_A_log, gdn_dt_bias, gdn_norm_w, gdn_w_out, pool_w, pool_scale, ffn_w_up, ffn_conv_w, ffn_conv_b, ffn_w_down, norm_final):
    p = _prep_params(norm_mix, norm_ffn, gdn_w_in, gdn_conv_w, gdn_A_log, gdn_dt_bias, gdn_norm_w, gdn_w_out,
                     pool_w, pool_scale, ffn_w_up, ffn_conv_w, ffn_conv_b, ffn_w_down, norm_final)
    bp, lp, _ = x_prompt.shape
    bs, ls, _ = x_sample.shape

    _, m_conv, m_s, m_pool, m_ffn = _trunk(
        meta_tokens.astype(F32), 1, N_META,
        jnp.zeros((SUB, QKV), F32), jnp.zeros((1, HEADS, DK, DV), F32), jnp.zeros((2 * SUB, D_MODEL), F32),
        (jnp.zeros((SUB, 2 * D_FF), F32), jnp.zeros((SUB, 2 * D_FF), F32)), 0, p)

    rep = lambda a: jnp.tile(a, (bp, 1))
    yp, p_conv, p_s, p_pool, p_ffn = _trunk(
        x_prompt.reshape(bp * lp, D_MODEL), bp, lp,
        rep(m_conv), jnp.broadcast_to(m_s, (bp, HEADS, DK, DV)), rep(m_pool), (rep(m_ffn[0]), rep(m_ffn[1])),
        N_META, p)

    tm = lambda a: jnp.swapaxes(a, 0, 1)
    ys, s_conv, s_s, s_pool, s_ffn = _trunk(
        x_sample, bs, ls,
        tm(state_gdn_conv[0]), state_gdn_rec[0], tm(state_pool[0]),
        (tm(state_ffn_conv[0]), tm(state_ffn_conv[1])), PAST_LEN, p)

    last = lambda a, b, r: a.reshape(b, -1, a.shape[-1])[:, -r:]
    return (yp.reshape(bp, lp, D_MODEL), ys,
            last(p_conv, bp, CONV_W - 1)[None], p_s[None], last(p_pool, bp, POOL_BUF)[None],
            jnp.stack([last(p_ffn[0], bp, FFN_CONV_W - 1), last(p_ffn[1], bp, FFN_CONV_W - 1)]),
            tm(s_conv)[None], s_s[None], tm(s_pool)[None],
            jnp.stack([tm(s_ffn[0]), tm(s_ffn[1])]))
```

```python
import functools

import jax
import jax.numpy as jnp
from jax import lax
from jax.experimental import pallas as pl
from jax.experimental.pallas import tpu as pltpu

D_MODEL = 1024
N_META = 16
EPS = 1e-6
HEADS = 8
DK = 128
DV = 128
KEY = HEADS * DK
VAL = HEADS * DV
QKV = 2 * KEY + VAL
CONV_W = 4
GDN_CHUNK = 64
POOL_WINDOWS = (2, 4, 8, 16)
POOL_GW = D_MODEL // 4
POOL_BUF = 15
D_FF = 2816
FFN_CONV_W = 3
PAST_LEN = 16384

SUB = 8
LANE = 128
FFN_COLS = 256
ROW_TILE = 512
SHORT_ROW_TILE = 256
SHORT_SEQS = 8
VMEM_LIMIT = 56 * 1024 * 1024

F32 = jnp.float32
BF16 = jnp.bfloat16
HI = lax.Precision.HIGHEST


def _dot(a, b):
    return jnp.dot(a.astype(BF16), b.astype(BF16), preferred_element_type=F32)


def _dot_nt(a, b):
    return lax.dot_general(a.astype(BF16), b.astype(BF16), (((1,), (1,)), ((), ())), preferred_element_type=F32)


def _dot_tn(a, b):
    return lax.dot_general(a.astype(BF16), b.astype(BF16), (((0,), (0,)), ((), ())), preferred_element_type=F32)


def _rms(x, w):
    return x * lax.rsqrt(jnp.mean(x * x, axis=-1, keepdims=True) + EPS) * w


def _silu(x):
    return x * jax.nn.sigmoid(x)


def _softplus(x):
    return jnp.maximum(x, 0.0) + jnp.log1p(jnp.exp(-jnp.abs(x)))


def _conv_taps(u, prev, width, nseq):
    rows = u.shape[0]
    xcat = jnp.concatenate([prev, u], axis=0)
    if nseq:
        return [xcat[j * nseq:j * nseq + rows] for j in range(width)], xcat[rows:]
    taps = [pltpu.roll(xcat, width - 1 - j, 0)[SUB:] for j in range(width - 1)] + [u]
    return taps, u[rows - SUB:]


class _Layout:
    def __init__(self, nseq, nt, shared=False):
        self.nseq = nseq
        self.nt = nt
        self.shared = shared

    def load(self, ref, cols=slice(None)):
        if self.nseq:
            return jnp.concatenate([ref[:, t, cols] for t in range(SUB)], axis=0)
        return ref[:, cols]

    def store(self, ref, val, cols=slice(None)):
        val = val.astype(ref.dtype)
        if self.nseq:
            for t in range(SUB):
                ref[:, t, cols] = val[t * self.nseq:(t + 1) * self.nseq]
        else:
            ref[:, cols] = val

    def load_state(self, ref, cols=slice(None)):
        return jnp.concatenate([ref[i, :, cols] for i in range(ref.shape[0])], axis=0)

    def store_state(self, ref, val, cols=slice(None)):
        for i in range(ref.shape[0]):
            ref[i, :, cols] = val[i * self.nseq:(i + 1) * self.nseq]

    def init_carry(self, carry_ref, prev_ref):
        if not self.nseq and self.nt > 1:
            @pl.when(pl.program_id(1) == 0)
            def _():
                carry_ref[...] = prev_ref[...]

    def conv(self, u, w_ref, prev_ref, carry_ref, new_ref, cols, width):
        if self.nseq:
            prev = self.load_state(prev_ref, cols)
        else:
            prev = (carry_ref if self.nt > 1 else prev_ref)[:, cols]
        taps, new = _conv_taps(u, prev, width, self.nseq)
        out = w_ref[0:1, cols] * taps[0]
        for j in range(1, width):
            out = out + w_ref[j:j + 1, cols] * taps[j]
        if self.nseq:
            self.store_state(new_ref, new, cols)
        else:
            new_ref[:, cols] = new
            if self.nt > 1:
                carry_ref[:, cols] = new
        return out

    def row_spec(self, rows, c):
        if self.nseq:
            return pl.BlockSpec((self.nseq, SUB, c), lambda b, t: (b, 0, 0))
        nt = self.nt
        return pl.BlockSpec((rows, c), lambda b, t: (b * nt + t, 0))

    def row_shape(self, n, c, dtype=F32):
        return jax.ShapeDtypeStruct((n // SUB, SUB, c) if self.nseq else (n, c), dtype)

    def state_spec(self, arr, nb):
        if self.nseq:
            return pl.BlockSpec((arr.shape[0], self.nseq, arr.shape[2]), lambda b, t: (0, b, 0))
        if self.shared:
            return pl.BlockSpec(arr.shape, lambda b, t: (0, 0))
        return pl.BlockSpec((arr.shape[0] // nb, arr.shape[1]), lambda b, t: (b, 0))

    def new_state(self, arr, nb):
        if self.nseq:
            return self.state_spec(arr, nb), jax.ShapeDtypeStruct(arr.shape, F32)
        tile = arr.shape[0] if self.shared else arr.shape[0] // nb
        return (pl.BlockSpec((tile, arr.shape[1]), lambda b, t: (b, 0)),
                jax.ShapeDtypeStruct((tile * nb, arr.shape[1]), F32))


def _proj_kernel(x_ref, prev_ref, nw_ref, wqkv_ref, wg_ref, cw_ref, alog_ref, dtb_ref,
                 q_ref, k_ref, v_ref, z_ref, g_ref, cnew_ref, carry_ref, *, lay):
    h = _rms(lay.load(x_ref), nw_ref[...]).astype(BF16)
    lay.store(z_ref, _dot(h, wqkv_ref[:, QKV:QKV + VAL]))
    lg = _dot(h, wg_ref[...])
    beta = jax.nn.sigmoid(lg)
    g = -jnp.exp(alog_ref[...]) * _softplus(lg + dtb_ref[...])
    lane = lax.broadcasted_iota(jnp.int32, lg.shape, 1)
    lay.store(g_ref, jnp.where(lane < HEADS, beta, g))

    lay.init_carry(carry_ref, prev_ref)
    for part, out_ref in enumerate((q_ref, k_ref, v_ref)):
        cols = slice(part * KEY, (part + 1) * KEY)
        raw = _dot(h, wqkv_ref[:, cols])
        act = _silu(lay.conv(raw, cw_ref, prev_ref, carry_ref, cnew_ref, cols, CONV_W))
        if part == 2:
            lay.store(out_ref, act)
            continue
        for hd in range(HEADS):
            hs = slice(hd * DK, (hd + 1) * DK)
            a = act[:, hs]
            n = a * lax.rsqrt(jnp.sum(a * a, axis=-1, keepdims=True) + EPS)
            lay.store(out_ref, n * (DK ** -0.5) if part == 0 else n, hs)


def _const_spec(shape):
    nd = len(shape)
    return pl.BlockSpec(shape, lambda b, t: (0,) * nd, pipeline_mode=pl.Buffered(1))


def _layer_spec(arr, layer):
    tail = arr.shape[1:]
    return pl.BlockSpec((None,) + tail, lambda b, t: (layer,) + (0,) * len(tail), pipeline_mode=pl.Buffered(1))


_PARAMS_2D = pltpu.CompilerParams(dimension_semantics=("arbitrary", "arbitrary"), vmem_limit_bytes=VMEM_LIMIT)


def _proj(x, prev, p, qkv_dtype, *, n, nb, nt, rows, lay):
    row_spec = functools.partial(lay.row_spec, rows)
    row_shape = functools.partial(lay.row_shape, n)
    qkv_shape = functools.partial(lay.row_shape, n, dtype=qkv_dtype)
    new_spec, new_shape = lay.new_state(prev, nb)
    return pl.pallas_call(
        functools.partial(_proj_kernel, lay=lay),
        grid=(nb, nt),
        in_specs=[row_spec(D_MODEL), lay.state_spec(prev, nb), _layer_spec(p["norm_mix"], 0),
                  _layer_spec(p["w_in"], 0), _const_spec((D_MODEL, LANE)),
                  _layer_spec(p["gdn_conv_w"], 0), _const_spec((1, LANE)), _const_spec((1, LANE))],
        out_specs=[row_spec(KEY), row_spec(KEY), row_spec(VAL), row_spec(VAL), row_spec(LANE), new_spec],
        out_shape=[qkv_shape(KEY), qkv_shape(KEY), qkv_shape(VAL), row_shape(VAL), row_shape(LANE), new_shape],
        scratch_shapes=[pltpu.VMEM((SUB, QKV), F32)],
        compiler_params=_PARAMS_2D,
        name="gdn_proj",
    )(x, prev, p["norm_mix"], p["w_in"], p["w_g"], p["gdn_conv_w"], p["a_log"], p["dt_bias"])


def _chunk_terms(units, same, n_sq):
    c = units[0][0].shape[0]
    ri = lax.broadcasted_iota(jnp.int32, (c, c), 0)
    ci = lax.broadcasted_iota(jnp.int32, (c, c), 1)
    eye_b = ri == ci
    eye = eye_b.astype(F32)
    causal = same & (ri >= ci)
    pre = []
    for q, k, v, beta_b, g_b in units:
        g_row = jnp.sum(jnp.where(eye_b, g_b, 0.0), axis=0, keepdims=True)
        cum_row = jnp.sum(jnp.where(same & (ri <= ci), g_b, 0.0), axis=0, keepdims=True)
        cum_col = jnp.sum(jnp.where(eye_b, cum_row, 0.0), axis=1, keepdims=True)
        g_tot = jnp.sum(jnp.where(same, g_row, 0.0), axis=1, keepdims=True)
        decay = jnp.exp(jnp.where(causal, cum_col - cum_row, -jnp.inf))
        pre.append((decay, jnp.exp(cum_col), jnp.exp(g_tot - cum_col), g_tot))
    kk = [_dot_nt(k, k) for _, k, _, _, _ in units]
    qk = [_dot_nt(q, k) for q, k, _, _, _ in units]
    low = [jnp.where(same & (ri > ci), un[3] * kk_u * pr[0], 0.0) for un, kk_u, pr in zip(units, kk, pre)]
    x = [eye - lw for lw in low]
    pw = low
    for _ in range(n_sq):
        pw = [jnp.dot(m, m, precision=HI, preferred_element_type=F32) for m in pw]
        x = [xi + jnp.dot(xi, m, precision=HI, preferred_element_type=F32) for xi, m in zip(x, pw)]
    u = [_dot(t, un[2] * un[3][:, 0:1]) for t, un in zip(x, units)]
    w = [_dot(t, un[1] * (un[3][:, 0:1] * pr[1])) for t, un, pr in zip(x, units, pre)]
    return [(u_u, w_u, qk_u * pr[0], un[0] * pr[1], un[1] * pr[2], pr[3])
            for u_u, w_u, qk_u, un, pr in zip(u, w, qk, units, pre)]


def _delta_long_kernel(q_ref, k_ref, v_ref, g_ref, s0_ref, o_ref, s_ref, *, chunk, n_chunks, n_sq):
    t = pl.program_id(1)

    @pl.when(t == 0)
    def _():
        s_ref[...] = s0_ref[...]

    same = jnp.full((chunk, chunk), True)

    def body(c, carry):
        rows = pl.ds(pl.multiple_of(c * chunk, chunk), chunk)
        gates = g_ref[rows, :]
        heads = range(HEADS)
        hcols = [slice(hd * DK, (hd + 1) * DK) for hd in heads]
        terms = _chunk_terms(
            [(q_ref[rows, hs], k_ref[rows, hs], v_ref[rows, hs],
              jnp.broadcast_to(gates[:, hd:hd + 1], (chunk, chunk)),
              jnp.broadcast_to(gates[:, HEADS + hd:HEADS + hd + 1], (chunk, chunk)))
             for hd, hs in zip(heads, hcols)], same, n_sq)
        s = [s_ref[0, hd] for hd in heads]
        v_new = [tm[0] - _dot(tm[1], s_h) for tm, s_h in zip(terms, s)]
        qs = [_dot(tm[3], s_h) for tm, s_h in zip(terms, s)]
        av = [_dot(tm[2], vn) for tm, vn in zip(terms, v_new)]
        kv = [_dot_tn(tm[4], vn) for tm, vn in zip(terms, v_new)]
        for hd in heads:
            o_ref[rows, hcols[hd]] = qs[hd] + av[hd]
            s_ref[0, hd] = s[hd] * jnp.exp(terms[hd][5][0:1, :]) + kv[hd]
        return carry

    lax.fori_loop(0, n_chunks, body, 0)


def _delta_long(q, k, v, gates, s0, *, n, nb, nt, rows, lay, chunk):
    row_spec = functools.partial(lay.row_spec, rows)
    s_spec = pl.BlockSpec((1, HEADS, DK, DV), lambda b, t: (b, 0, 0, 0))
    n_sq = chunk.bit_length() - 2
    kern = functools.partial(_delta_long_kernel, chunk=chunk, n_chunks=rows // chunk, n_sq=n_sq)
    return pl.pallas_call(
        kern,
        grid=(nb, nt),
        in_specs=[row_spec(KEY), row_spec(KEY), row_spec(VAL), row_spec(LANE), s_spec],
        out_specs=[row_spec(VAL), s_spec],
        out_shape=[jax.ShapeDtypeStruct((n, VAL), F32), jax.ShapeDtypeStruct(s0.shape, F32)],
        compiler_params=_PARAMS_2D,
        name="gdn_delta_long",
    )(q, k, v, gates, s0)


UNITS = LANE
TILE_CHUNKS = UNITS // HEADS
TILE_ROWS = TILE_CHUNKS * GDN_CHUNK


def _col(x, lane, width):
    return jnp.broadcast_to(x[:, lane:lane + 1], (x.shape[0], width))


def _delta_tile_kernel(q_ref, k_ref, v_ref, g_ref, s0_ref, o_ref, s_ref,
                       low_scr, lowt_scr, tcol_scr, tinv_scr, u_scr, kg_scr, w_scr, qd_scr, kd_scr, a_scr, gt_scr,
                       beta_scr):
    @pl.when(pl.program_id(1) == 0)
    def _():
        s_ref[...] = s0_ref[...]

    chunk = GDN_CHUNK
    half = chunk // 2
    row_id = lax.broadcasted_iota(jnp.int32, (chunk, LANE), 0)
    lane_id = lax.broadcasted_iota(jnp.int32, (chunk, LANE), 1)
    col_id = lane_id % chunk
    first_half = lax.broadcasted_iota(jnp.int32, (half, LANE), 1) < chunk
    heads = range(HEADS)
    hcols = [slice(h * DK, (h + 1) * DK) for h in heads]

    def phase_1a(c, carry):
        rows = pl.ds(pl.multiple_of(c * chunk, chunk), chunk)
        gates = g_ref[rows, :]
        cum = gates
        d = 1
        while d < chunk:
            cum = cum + jnp.where(row_id >= d, pltpu.roll(cum, d, 0), 0.0)
            d *= 2
        tot = cum[chunk - 1:chunk, :]
        cum_t = jnp.concatenate([cum, cum], axis=0).T
        beta_scr[...] = jnp.where(lane_id // HEADS == c, pltpu.roll(gates, c * HEADS, 1), beta_scr[...])
        def scores(h):
            qh, kh = q_ref[rows, hcols[h]], k_ref[rows, hcols[h]]
            return qh, kh, _dot_nt(jnp.concatenate([qh, kh], axis=0), jnp.concatenate([kh, kh], axis=0))

        ahead = scores(0)
        for h in heads:
            qh, kh, sc = ahead
            if h + 1 < HEADS:
                ahead = scores(h + 1)
            cum_b = _col(cum, HEADS + h, DK)
            tot_b = jnp.broadcast_to(tot[:, HEADS + h:HEADS + h + 1], (chunk, DK))
            diff = cum_b - jnp.broadcast_to(cum_t[HEADS + h:HEADS + h + 1, :], (chunk, LANE))
            decay = jnp.exp(jnp.where(row_id >= col_id, diff, -jnp.inf))
            unit = c * HEADS + h
            low = jnp.where(row_id > col_id, sc[chunk:] * decay, 0.0)
            low_scr[pl.ds(unit, half, stride=UNITS), :] = jnp.where(first_half, low[:half], low[half:])
            a_scr[unit] = (sc[:chunk] * decay)[:, :chunk].astype(BF16)
            e1 = jnp.exp(cum_b)
            kf = kh.astype(F32)
            kg_scr[rows, hcols[h]] = (kf * e1).astype(BF16)
            qd_scr[rows, hcols[h]] = (qh.astype(F32) * e1).astype(BF16)
            kd_scr[rows, hcols[h]] = (kf * jnp.exp(tot_b - cum_b)).astype(BF16)
        e_tot = jnp.exp(tot)
        gt_scr[pl.ds(pl.multiple_of(c * HEADS, HEADS), HEADS), :] = jnp.concatenate(
            [jnp.broadcast_to(e_tot[:, HEADS + h:HEADS + h + 1], (1, LANE)) for h in heads], axis=0)
        return carry

    beta_scr[...] = jnp.zeros(beta_scr.shape, F32)
    lax.fori_loop(0, TILE_CHUNKS, phase_1a, 0)

    for i2 in range(half):
        slab = low_scr[pl.ds(i2 * UNITS, UNITS), :].T
        lowt_scr[i2] = slab[:chunk]
        lowt_scr[i2 + half] = slab[chunk:]
    sub = lax.broadcasted_iota(jnp.int32, (SUB, LANE), 0)
    zero = jnp.zeros((SUB, LANE), F32)
    for i in range(chunk):
        nblk = i // SUB + 1
        acc = [[zero, zero] for _ in range(nblk)]
        if i:
            beta_i = jnp.broadcast_to(beta_scr[i:i + 1, :], (SUB, LANE))
        for j in range(i):
            lrow = jnp.broadcast_to(lowt_scr[i, j:j + 1, :], (SUB, LANE))
            for cb in range(j // SUB + 1):
                acc[cb][j % 2] = acc[cb][j % 2] + lrow * tcol_scr[j, cb * SUB:(cb + 1) * SUB, :]
        for cb in range(chunk // SUB):
            blk = zero
            if i and cb < nblk:
                blk = -(beta_i * (acc[cb][0] + acc[cb][1]))
            if cb == i // SUB:
                blk = blk + (sub == i % SUB).astype(F32)
            tcol_scr[i, cb * SUB:(cb + 1) * SUB, :] = blk
    for i2 in range(half):
        tinv_scr[pl.ds(i2 * UNITS, UNITS), :] = jnp.concatenate([tcol_scr[i2], tcol_scr[i2 + half]], axis=0).T

    def uw_compute(c):
        rows = pl.ds(pl.multiple_of(c * chunk, chunk), chunk)
        gates = g_ref[rows, :]
        gates_t = jnp.concatenate([gates, gates], axis=0).T
        out = []
        lhs = []
        for h in heads:
            packed = tinv_scr[pl.ds(c * HEADS + h, half, stride=UNITS), :]
            packed = packed * jnp.broadcast_to(gates_t[h:h + 1, :], (half, LANE))
            lhs.append(jnp.concatenate([jnp.where(first_half, packed, 0.0),
                                        jnp.where(first_half, 0.0, packed)], axis=0))
        for h in heads:
            vk = jnp.concatenate([v_ref[rows, hcols[h]], kg_scr[rows, hcols[h]]], axis=1)
            out.append(_dot(lhs[h], jnp.concatenate([vk, vk], axis=0)))
        return out

    def uw_store(c, uw):
        rows = pl.ds(pl.multiple_of(c * chunk, chunk), chunk)
        for h in heads:
            u_scr[rows, hcols[h]] = uw[h][:, :DV]
            w_scr[rows, hcols[h]] = uw[h][:, DV:].astype(BF16)

    uw_store(0, uw_compute(0))

    def phase_2(c, carry):
        nxt = jnp.minimum(c + 1, TILE_CHUNKS - 1)
        rows = pl.ds(pl.multiple_of(c * chunk, chunk), chunk)
        s = [s_ref[0, h] for h in heads]
        m1 = [_dot(jnp.concatenate([w_scr[rows, hcols[h]], qd_scr[rows, hcols[h]]], axis=0), s[h].astype(BF16))
              for h in heads]
        uw = uw_compute(nxt)
        vb = [(u_scr[rows, hcols[h]] - m1[h][:chunk]).astype(BF16) for h in heads]
        av = [_dot(a_scr[c * HEADS + h], vb[h]) for h in heads]
        kv = [_dot_tn(kd_scr[rows, hcols[h]], vb[h]) for h in heads]
        for h in heads:
            o_ref[rows, hcols[h]] = m1[h][chunk:] + av[h]
            s_ref[0, h] = s[h] * gt_scr[pl.ds(c * HEADS + h, 1), :] + kv[h]
        uw_store(nxt, uw)
        return carry

    lax.fori_loop(0, TILE_CHUNKS, phase_2, 0)


def _delta_tiles(q, k, v, gates, s0, *, nb, nt):
    n = q.shape[0]
    row_spec = lambda c: pl.BlockSpec((TILE_ROWS, c), lambda b, t: (b * nt + t, 0))
    s_spec = pl.BlockSpec((1, HEADS, DK, DV), lambda b, t: (b, 0, 0, 0))
    s0_spec = s_spec if s0.shape[0] == nb else pl.BlockSpec((1, HEADS, DK, DV), lambda b, t: (0, 0, 0, 0))
    packed = pltpu.VMEM((GDN_CHUNK // 2 * UNITS, LANE), F32)
    by_row = pltpu.VMEM((GDN_CHUNK, GDN_CHUNK, UNITS), F32)
    tile_bf16 = pltpu.VMEM((TILE_ROWS, KEY), BF16)
    return pl.pallas_call(
        _delta_tile_kernel,
        grid=(nb, nt),
        in_specs=[row_spec(KEY), row_spec(KEY), row_spec(VAL), row_spec(LANE), s0_spec],
        out_specs=[row_spec(VAL), s_spec],
        out_shape=[jax.ShapeDtypeStruct((n, VAL), F32), jax.ShapeDtypeStruct((nb, HEADS, DK, DV), F32)],
        scratch_shapes=[packed, by_row, by_row, packed,
                        pltpu.VMEM((TILE_ROWS, VAL), F32), tile_bf16, tile_bf16, tile_bf16, tile_bf16,
                        pltpu.VMEM((UNITS, GDN_CHUNK, GDN_CHUNK), BF16), pltpu.VMEM((UNITS, LANE), F32),
                        pltpu.VMEM((GDN_CHUNK, UNITS), F32)],
        compiler_params=_PARAMS_2D,
        name="gdn_delta_tiles",
    )(q, k, v, gates, s0)


def _delta_short_kernel(q_ref, k_ref, v_ref, g_ref, s0_ref, o_ref, s_ref, *, nseq):
    c = HEADS * SUB
    ri = lax.broadcasted_iota(jnp.int32, (c, c), 0)
    ci = lax.broadcasted_iota(jnp.int32, (c, c), 1)
    same = (ri // SUB) == (ci // SUB)
    rh = lax.broadcasted_iota(jnp.int32, (2 * c, KEY), 0) % c // SUB
    ch = lax.broadcasted_iota(jnp.int32, (2 * c, KEY), 1) // DK
    head_cols2 = rh == ch
    head_cols = head_cols2[:c]
    heads = range(HEADS)
    seqs = range(nseq)
    rows = [slice(sq * SUB, (sq + 1) * SUB) for sq in seqs]

    def unit(sq):
        gates = g_ref[rows[sq], :]
        heads_to_rows = lambda ref: jnp.concatenate(
            [ref[rows[sq], hd * DK:(hd + 1) * DK] for hd in heads], axis=0)
        gate_rows = lambda lane0: jnp.concatenate(
            [jnp.broadcast_to(gates[:, lane0 + hd:lane0 + hd + 1], (SUB, c)) for hd in heads], axis=0)
        return heads_to_rows(q_ref), heads_to_rows(k_ref), heads_to_rows(v_ref), gate_rows(0), gate_rows(HEADS)

    terms = _chunk_terms([unit(sq) for sq in seqs], same, 2)
    spread = lambda m: jnp.where(head_cols if m.shape[0] == c else head_cols2,
                                 jnp.concatenate([m] * HEADS, axis=1), 0.0)
    s = [s0_ref[sq].reshape(KEY, DV) for sq in seqs]
    ws = [_dot(spread(jnp.concatenate([tm[1], tm[3]], axis=0)), s_q) for tm, s_q in zip(terms, s)]
    v_new = [tm[0] - m[:c] for tm, m in zip(terms, ws)]
    qs = [m[c:] for m in ws]
    av = [_dot(tm[2], vn) for tm, vn in zip(terms, v_new)]
    upd = [_dot_tn(spread(tm[4]), vn) for tm, vn in zip(terms, v_new)]
    for sq in seqs:
        o = qs[sq] + av[sq]
        o_ref[rows[sq], :] = jnp.concatenate([o[hd * SUB:(hd + 1) * SUB] for hd in heads], axis=1)
        for hd in heads:
            ks = slice(hd * DK, (hd + 1) * DK)
            s_ref[sq, hd] = s[sq][ks] * jnp.exp(terms[sq][5][hd * SUB:hd * SUB + 1, :]) + upd[sq][ks]


def _delta_short(q, k, v, gates, s0, *, nseq):
    n = q.shape[0]
    nb = n // (nseq * SUB)
    row_spec = lambda c: pl.BlockSpec((nseq * SUB, c), lambda b: (b, 0))
    s_spec = pl.BlockSpec((nseq, HEADS, DK, DV), lambda b: (b, 0, 0, 0))
    return pl.pallas_call(
        functools.partial(_delta_short_kernel, nseq=nseq),
        grid=(nb,),
        in_specs=[row_spec(KEY), row_spec(KEY), row_spec(VAL), row_spec(LANE), s_spec],
        out_specs=[row_spec(VAL), s_spec],
        out_shape=[jax.ShapeDtypeStruct((n, VAL), F32), jax.ShapeDtypeStruct(s0.shape, F32)],
        compiler_params=pltpu.CompilerParams(dimension_semantics=("arbitrary",),
                                             vmem_limit_bytes=VMEM_LIMIT),
        name="gdn_delta_short",
    )(q, k, v, gates, s0)


def _conv_ffn(h, lay, prev_ref, carry_ref, new_ref, wup_ref, cw_ref, cb_ref, wdown_ref, act_ref):
    lay.init_carry(carry_ref, prev_ref)
    for c in range(D_FF // FFN_COLS):
        halves = []
        for part in range(2):
            lo = part * D_FF + c * FFN_COLS
            cols = slice(lo, lo + FFN_COLS)
            u = _dot(h, wup_ref[:, cols])
            halves.append(lay.conv(u, cw_ref, prev_ref, carry_ref, new_ref, cols, FFN_CONV_W) + cb_ref[:, cols])
        act_ref[:, c * FFN_COLS:(c + 1) * FFN_COLS] = (_silu(halves[0]) * halves[1]).astype(BF16)
    return _dot(act_ref[...], wdown_ref[...])


def _mixffn_kernel(x_ref, o_ref, z_ref, gnw_ref, wout_ref, nf_ref, prev_ref, wup_ref, cw_ref, cb_ref, wdown_ref,
                   y_ref, new_ref, carry_ref, act_ref, gated_ref, *, lay):
    for hd in range(HEADS):
        hs = slice(hd * DV, (hd + 1) * DV)
        o = lay.load(o_ref, hs)
        o = o * lax.rsqrt(jnp.mean(o * o, axis=-1, keepdims=True) + EPS) * gnw_ref[...]
        gated_ref[:, hs] = (o * _silu(lay.load(z_ref, hs))).astype(BF16)
    x1 = lay.load(x_ref) + _dot(gated_ref[...], wout_ref[...])
    h = _rms(x1, nf_ref[...]).astype(BF16)
    f = _conv_ffn(h, lay, prev_ref, carry_ref, new_ref, wup_ref, cw_ref, cb_ref, wdown_ref, act_ref)
    lay.store(y_ref, x1 + f)


def _ffn_specs(lay, prev, nb, p, layer):
    return [lay.state_spec(prev, nb)] + [_layer_spec(a, layer) for a in _ffn_args(p)]


def _ffn_args(p):
    return [p["w_up"], p["ffn_conv_w"], p["ffn_conv_b"], p["w_down"]]


def _mixffn(x, o, z, prev, p, *, n, nb, nt, rows, lay):
    row_spec = functools.partial(lay.row_spec, rows)
    return pl.pallas_call(
        functools.partial(_mixffn_kernel, lay=lay),
        grid=(nb, nt),
        in_specs=[row_spec(D_MODEL), row_spec(VAL), row_spec(VAL), _const_spec((1, DV)),
                  _layer_spec(p["w_out"], 0), _layer_spec(p["norm_ffn"], 0)] + _ffn_specs(lay, prev, nb, p, 0),
        out_specs=[row_spec(D_MODEL), lay.new_state(prev, nb)[0]],
        out_shape=[lay.row_shape(n, D_MODEL), lay.new_state(prev, nb)[1]],
        scratch_shapes=[pltpu.VMEM((SUB, 2 * D_FF), F32), pltpu.VMEM((rows, D_FF), BF16),
                        pltpu.VMEM((rows, VAL), BF16)],
        compiler_params=_PARAMS_2D,
        name="gdn_out_ffn",
    )(x, o, z, p["gdn_norm_w"], p["w_out"], p["norm_ffn"], prev, *_ffn_args(p))


def _window_sum(hcat, level, step):
    s = hcat
    for lv in range(level):
        d = step << lv
        s = s[d:] + s[:s.shape[0] - d]
    return s


def _poolffn_kernel(x_ref, pprev_ref, nm_ref, pw_ref, ps_ref, nf_ref, prev_ref, wup_ref, cw_ref, cb_ref, wdown_ref,
                    nfin_ref, y_ref, pnew_ref, new_ref, pcarry_ref, carry_ref, act_ref, mix_ref, *, lay, start_pos):
    x = lay.load(x_ref)
    rows = x.shape[0]
    h = _rms(x, nm_ref[...])
    row = lax.broadcasted_iota(jnp.int32, (rows, 1), 0)
    if lay.nseq:
        step = lay.nseq
        pos = start_pos + row // step
        hcat = jnp.concatenate([lay.load_state(pprev_ref), h], axis=0)
        lay.store_state(pnew_ref, hcat[rows:])
        first = POOL_BUF * step
    else:
        step = 1
        pos = start_pos + pl.program_id(1) * rows + row
        if lay.nt > 1:
            @pl.when(pl.program_id(1) == 0)
            def _():
                pcarry_ref[...] = pprev_ref[...]
            pprev = pcarry_ref[...]
        else:
            pprev = pprev_ref[...]
        hcat = jnp.concatenate([pprev, h], axis=0)
        pnew_ref[...] = hcat[rows:]
        if lay.nt > 1:
            pcarry_ref[...] = hcat[rows:]
        first = 2 * SUB
    for gi, win in enumerate(POOL_WINDOWS):
        cols = slice(gi * POOL_GW, (gi + 1) * POOL_GW)
        if lay.nseq:
            lo = first - (win - 1) * step
            wsum = _window_sum(hcat[:, cols], gi + 1, step)[lo:lo + rows]
        else:
            s = hcat[:, cols]
            for lv in range(gi + 1):
                s = s + pltpu.roll(s, 1 << lv, 0)
            wsum = s[first:]
        cnt = jnp.minimum(win, pos + 1).astype(F32)
        pooled = wsum / cnt - h[:, cols]
        mix_ref[:, cols] = _dot(pooled.astype(BF16), pw_ref[gi])
    x3 = x + mix_ref[...] * ps_ref[...]
    hf = _rms(x3, nf_ref[...]).astype(BF16)
    f = _conv_ffn(hf, lay, prev_ref, carry_ref, new_ref, wup_ref, cw_ref, cb_ref, wdown_ref, act_ref)
    lay.store(y_ref, _rms(x3 + f, nfin_ref[...]))


def _poolffn(x, pool_prev, prev, p, *, n, nb, nt, rows, lay, start_pos):
    row_spec = functools.partial(lay.row_spec, rows)
    return pl.pallas_call(
        functools.partial(_poolffn_kernel, lay=lay, start_pos=start_pos),
        grid=(nb, nt),
        in_specs=[row_spec(D_MODEL), lay.state_spec(pool_prev, nb),
                  _layer_spec(p["norm_mix"], 1), _layer_spec(p["pool_w"], 0), _const_spec((1, D_MODEL)),
                  _layer_spec(p["norm_ffn"], 1)] + _ffn_specs(lay, prev, nb, p, 1) + [_const_spec((1, D_MODEL))],
        out_specs=[row_spec(D_MODEL), lay.new_state(pool_prev, nb)[0], lay.new_state(prev, nb)[0]],
        out_shape=[lay.row_shape(n, D_MODEL), lay.new_state(pool_prev, nb)[1], lay.new_state(prev, nb)[1]],
        scratch_shapes=[pltpu.VMEM((2 * SUB, D_MODEL), F32), pltpu.VMEM((SUB, 2 * D_FF), F32),
                        pltpu.VMEM((rows, D_FF), BF16), pltpu.VMEM((rows, D_MODEL), F32)],
        compiler_params=_PARAMS_2D,
        name="pool_ffn",
    )(x, pool_prev, p["norm_mix"], p["pool_w"], p["pool_scale"], p["norm_ffn"], prev, *_ffn_args(p),
      p["norm_final"])


def _trunk(x, n_seq, seq_len, conv_prev, s0, pool_prev, ffn_prev, start_pos, p, shared_state=False):
    n = n_seq * seq_len
    if seq_len == SUB:
        rows = min(SHORT_ROW_TILE, n)
        geom = dict(n=n, nb=n // rows, nt=1, rows=rows, lay=_Layout(rows // SUB, 1))
    else:
        rows = min(ROW_TILE, seq_len)
        geom = dict(n=n, nb=n_seq, nt=seq_len // rows, rows=rows, lay=_Layout(0, seq_len // rows, shared_state))
    tiled = seq_len % TILE_ROWS == 0
    q, k, v, z, gates, conv_new = _proj(x, conv_prev, p, BF16 if tiled else F32, **geom)
    if seq_len == SUB:
        flat = lambda a: a.reshape(n, a.shape[-1])
        o, s_new = _delta_short(flat(q), flat(k), flat(v), flat(gates), s0, nseq=min(SHORT_SEQS, n_seq))
        o = o.reshape(n_seq, SUB, VAL)
    elif tiled:
        o, s_new = _delta_tiles(q, k, v, gates, s0, nb=n_seq, nt=seq_len // TILE_ROWS)
    else:
        o, s_new = _delta_long(q, k, v, gates, s0, chunk=min(GDN_CHUNK, seq_len), **geom)
    x2, ffn_new0 = _mixffn(x, o, z, ffn_prev[0], p, **geom)
    y, pool_new, ffn_new1 = _poolffn(x2, pool_prev, ffn_prev[1], p, start_pos=start_pos, **geom)
    return y, conv_new, s_new, pool_new, (ffn_new0, ffn_new1)


def _prep_params(norm_mix, norm_ffn, gdn_w_in, gdn_conv_w, gdn_A_log, gdn_dt_bias, gdn_norm_w, gdn_w_out,
                 pool_w, pool_scale, ffn_w_up, ffn_conv_w, ffn_conv_b, ffn_w_down, norm_final):
    gate_pad = ((0, 0), (0, LANE - 2 * HEADS))
    head_pad = ((0, 0), (HEADS, LANE - 2 * HEADS))
    row = lambda a: a.reshape(a.shape[0], 1, a.shape[1])
    return {
        "norm_mix": row(norm_mix), "norm_ffn": row(norm_ffn),
        "w_in": gdn_w_in.astype(BF16),
        "w_g": jnp.pad(gdn_w_in[0][:, QKV + VAL:], gate_pad).astype(BF16),
        "gdn_conv_w": gdn_conv_w,
        "a_log": jnp.pad(gdn_A_log[0:1], head_pad),
        "dt_bias": jnp.pad(gdn_dt_bias[0:1], head_pad),
        "gdn_norm_w": gdn_norm_w[0:1],
        "w_out": gdn_w_out.astype(BF16),
        "pool_w": pool_w.astype(BF16),
        "pool_scale": pool_scale[0:1],
        "w_up": ffn_w_up.astype(BF16), "ffn_conv_w": ffn_conv_w, "ffn_conv_b": row(ffn_conv_b),
        "w_down": ffn_w_down.astype(BF16),
        "norm_final": norm_final[None],
    }


def kernel(x_prompt, x_sample, state_gdn_conv, state_gdn_rec, state_pool, state_ffn_conv, meta_tokens, norm_mix, norm_ffn, gdn_w_in, gdn_conv_w, gdn_A_log, gdn_dt_bias, gdn_norm_w, gdn_w_out, pool_w, pool_scale, ffn_w_up, ffn_conv_w, ffn_conv_b, ffn_w_down, norm_final):
    p = _prep_params(norm_mix, norm_ffn, gdn_w_in, gdn_conv_w, gdn_A_log, gdn_dt_bias, gdn_norm_w, gdn_w_out,
                     pool_w, pool_scale, ffn_w_up, ffn_conv_w, ffn_conv_b, ffn_w_down, norm_final)
    bp, lp, _ = x_prompt.shape
    bs, ls, _ = x_sample.shape

    _, m_conv, m_s, m_pool, m_ffn = _trunk(
        meta_tokens.astype(F32), 1, N_META,
        jnp.zeros((SUB, QKV), F32), jnp.zeros((1, HEADS, DK, DV), F32), jnp.zeros((2 * SUB, D_MODEL), F32),
        (jnp.zeros((SUB, 2 * D_FF), F32), jnp.zeros((SUB, 2 * D_FF), F32)), 0, p)

    yp, p_conv, p_s, p_pool, p_ffn = _trunk(
        x_prompt.reshape(bp * lp, D_MODEL), bp, lp, m_conv, m_s, m_pool, m_ffn, N_META, p, shared_state=True)

    tm = lambda a: jnp.swapaxes(a, 0, 1)
    ys, s_conv, s_s, s_pool, s_ffn = _trunk(
        x_sample, bs, ls,
        tm(state_gdn_conv[0]), state_gdn_rec[0], tm(state_pool[0]),
        (tm(state_ffn_conv[0]), tm(state_ffn_conv[1])), PAST_LEN, p)

    last = lambda a, b, r: a.reshape(b, -1, a.shape[-1])[:, -r:]
    return (yp.reshape(bp, lp, D_MODEL), ys,
            last(p_conv, bp, CONV_W - 1)[None], p_s[None], last(p_pool, bp, POOL_BUF)[None],
            jnp.stack([last(p_ffn[0], bp, FFN_CONV_W - 1), last(p_ffn[1], bp, FFN_CONV_W - 1)]),
            tm(s_conv)[None], s_s[None], tm(s_pool)[None],
            jnp.stack([tm(s_ffn[0]), tm(s_ffn[1])]))
```

```python
import functools

import jax
import jax.numpy as jnp
from jax import lax
from jax.experimental import pallas as pl
from jax.experimental.pallas import tpu as pltpu

D_MODEL = 1024
N_META = 16
EPS = 1e-6
HEADS = 8
DK = 128
DV = 128
KEY = HEADS * DK
VAL = HEADS * DV
QKV = 2 * KEY + VAL
CONV_W = 4
GDN_CHUNK = 64
POOL_WINDOWS = (2, 4, 8, 16)
POOL_GW = D_MODEL // 4
POOL_BUF = 15
D_FF = 2816
FFN_CONV_W = 3
PAST_LEN = 16384

SUB = 8
LANE = 128
FFN_COLS = 256
PROJ_COLS = 256
ROW_TILE = 512
SHORT_ROW_TILE = 256
SHORT_SEQS = 8
VMEM_LIMIT = 56 * 1024 * 1024

F32 = jnp.float32
BF16 = jnp.bfloat16
HI = lax.Precision.HIGHEST


def _dot(a, b):
    return jnp.dot(a.astype(BF16), b.astype(BF16), preferred_element_type=F32)


def _dot_nt(a, b):
    return lax.dot_general(a.astype(BF16), b.astype(BF16), (((1,), (1,)), ((), ())), preferred_element_type=F32)


def _dot_tn(a, b):
    return lax.dot_general(a.astype(BF16), b.astype(BF16), (((0,), (0,)), ((), ())), preferred_element_type=F32)


def _rms(x, w):
    return x * lax.rsqrt(jnp.mean(x * x, axis=-1, keepdims=True) + EPS) * w


def _silu(x):
    return x * jax.nn.sigmoid(x)


def _softplus(x):
    return jnp.maximum(x, 0.0) + jnp.log1p(jnp.exp(-jnp.abs(x)))


def _conv_taps(u, prev, width, nseq):
    rows = u.shape[0]
    xcat = jnp.concatenate([prev, u], axis=0)
    if nseq:
        return [xcat[j * nseq:j * nseq + rows] for j in range(width)], xcat[rows:]
    taps = [pltpu.roll(xcat, width - 1 - j, 0)[SUB:] for j in range(width - 1)] + [u]
    return taps, u[rows - SUB:]


class _Layout:
    def __init__(self, nseq, nt, shared=False):
        self.nseq = nseq
        self.nt = nt
        self.shared = shared

    def load(self, ref, cols=slice(None)):
        if self.nseq:
            return jnp.concatenate([ref[:, t, cols] for t in range(SUB)], axis=0)
        return ref[:, cols]

    def store(self, ref, val, cols=slice(None)):
        val = val.astype(ref.dtype)
        if self.nseq:
            for t in range(SUB):
                ref[:, t, cols] = val[t * self.nseq:(t + 1) * self.nseq]
        else:
            ref[:, cols] = val

    def load_state(self, ref, cols=slice(None)):
        return jnp.concatenate([ref[i, :, cols] for i in range(ref.shape[0])], axis=0)

    def store_state(self, ref, val, cols=slice(None)):
        for i in range(ref.shape[0]):
            ref[i, :, cols] = val[i * self.nseq:(i + 1) * self.nseq]

    def init_carry(self, carry_ref, prev_ref):
        if not self.nseq and self.nt > 1:
            @pl.when(pl.program_id(1) == 0)
            def _():
                carry_ref[...] = prev_ref[...]

    def conv(self, u, w_ref, prev_ref, carry_ref, new_ref, cols, width):
        if self.nseq:
            prev = self.load_state(prev_ref, cols)
        else:
            prev = (carry_ref if self.nt > 1 else prev_ref)[:, cols]
        taps, new = _conv_taps(u, prev, width, self.nseq)
        out = w_ref[0:1, cols] * taps[0]
        for j in range(1, width):
            out = out + w_ref[j:j + 1, cols] * taps[j]
        if self.nseq:
            self.store_state(new_ref, new, cols)
        else:
            new_ref[:, cols] = new
            if self.nt > 1:
                carry_ref[:, cols] = new
        return out

    def row_spec(self, rows, c):
        if self.nseq:
            return pl.BlockSpec((self.nseq, SUB, c), lambda b, t: (b, 0, 0))
        nt = self.nt
        return pl.BlockSpec((rows, c), lambda b, t: (b * nt + t, 0))

    def row_shape(self, n, c, dtype=F32):
        return jax.ShapeDtypeStruct((n // SUB, SUB, c) if self.nseq else (n, c), dtype)

    def state_spec(self, arr, nb):
        if self.nseq:
            return pl.BlockSpec((arr.shape[0], self.nseq, arr.shape[2]), lambda b, t: (0, b, 0))
        if self.shared:
            return pl.BlockSpec(arr.shape, lambda b, t: (0, 0))
        return pl.BlockSpec((arr.shape[0] // nb, arr.shape[1]), lambda b, t: (b, 0))

    def new_state(self, arr, nb):
        if self.nseq:
            return self.state_spec(arr, nb), jax.ShapeDtypeStruct(arr.shape, F32)
        tile = arr.shape[0] if self.shared else arr.shape[0] // nb
        return (pl.BlockSpec((tile, arr.shape[1]), lambda b, t: (b, 0)),
                jax.ShapeDtypeStruct((tile * nb, arr.shape[1]), F32))


def _proj_kernel(x_ref, prev_ref, nw_ref, wqkv_ref, wg_ref, cw_ref, alog_ref, dtb_ref,
                 q_ref, k_ref, v_ref, z_ref, g_ref, cnew_ref, carry_ref, *, lay):
    h = _rms(lay.load(x_ref), nw_ref[...]).astype(BF16)
    lay.init_carry(carry_ref, prev_ref)
    blocks = []
    for c in range(QKV // PROJ_COLS):
        blocks.append(c * PROJ_COLS)
        if c % 3 == 2:
            blocks.append(QKV + (c // 3) * PROJ_COLS)

    def finish(lo, raw):
        cols = slice(lo, lo + PROJ_COLS)
        if lo >= QKV:
            lay.store(z_ref, raw, slice(lo - QKV, lo - QKV + PROJ_COLS))
            return
        part, off = divmod(lo, KEY)
        out_ref = (q_ref, k_ref, v_ref)[part]
        act = _silu(lay.conv(raw, cw_ref, prev_ref, carry_ref, cnew_ref, cols, CONV_W))
        if part == 2:
            lay.store(out_ref, act, slice(off, off + PROJ_COLS))
            return
        for hd in range(PROJ_COLS // DK):
            a = act[:, hd * DK:(hd + 1) * DK]
            n = a * lax.rsqrt(jnp.sum(a * a, axis=-1, keepdims=True) + EPS)
            lay.store(out_ref, n * (DK ** -0.5) if part == 0 else n, slice(off + hd * DK, off + (hd + 1) * DK))

    matmul = lambda lo: _dot(h, wqkv_ref[:, lo:lo + PROJ_COLS])
    ahead = matmul(blocks[0])
    for i, lo in enumerate(blocks):
        raw = ahead
        if i + 1 < len(blocks):
            ahead = matmul(blocks[i + 1])
        finish(lo, raw)
    lg = _dot(h, wg_ref[...])
    beta = jax.nn.sigmoid(lg)
    g = -jnp.exp(alog_ref[...]) * _softplus(lg + dtb_ref[...])
    lane = lax.broadcasted_iota(jnp.int32, lg.shape, 1)
    lay.store(g_ref, jnp.where(lane < HEADS, beta, g))


def _const_spec(shape):
    nd = len(shape)
    return pl.BlockSpec(shape, lambda b, t: (0,) * nd, pipeline_mode=pl.Buffered(1))


def _layer_spec(arr, layer):
    tail = arr.shape[1:]
    return pl.BlockSpec((None,) + tail, lambda b, t: (layer,) + (0,) * len(tail), pipeline_mode=pl.Buffered(1))


_PARAMS_2D = pltpu.CompilerParams(dimension_semantics=("arbitrary", "arbitrary"), vmem_limit_bytes=VMEM_LIMIT)


def _proj(x, prev, p, qkv_dtype, *, n, nb, nt, rows, lay):
    row_spec = functools.partial(lay.row_spec, rows)
    row_shape = functools.partial(lay.row_shape, n)
    qkv_shape = functools.partial(lay.row_shape, n, dtype=qkv_dtype)
    new_spec, new_shape = lay.new_state(prev, nb)
    return pl.pallas_call(
        functools.partial(_proj_kernel, lay=lay),
        grid=(nb, nt),
        in_specs=[row_spec(D_MODEL), lay.state_spec(prev, nb), _layer_spec(p["norm_mix"], 0),
                  _layer_spec(p["w_in"], 0), _const_spec((D_MODEL, LANE)),
                  _layer_spec(p["gdn_conv_w"], 0), _const_spec((1, LANE)), _const_spec((1, LANE))],
        out_specs=[row_spec(KEY), row_spec(KEY), row_spec(VAL), row_spec(VAL), row_spec(LANE), new_spec],
        out_shape=[qkv_shape(KEY), qkv_shape(KEY), qkv_shape(VAL), row_shape(VAL), row_shape(LANE), new_shape],
        scratch_shapes=[pltpu.VMEM((SUB, QKV), F32)],
        compiler_params=_PARAMS_2D,
        name="gdn_proj",
    )(x, prev, p["norm_mix"], p["w_in"], p["w_g"], p["gdn_conv_w"], p["a_log"], p["dt_bias"])


def _chunk_terms(units, same, n_sq):
    c = units[0][0].shape[0]
    ri = lax.broadcasted_iota(jnp.int32, (c, c), 0)
    ci = lax.broadcasted_iota(jnp.int32, (c, c), 1)
    eye_b = ri == ci
    eye = eye_b.astype(F32)
    causal = same & (ri >= ci)
    pre = []
    for q, k, v, beta_b, g_b in units:
        g_row = jnp.sum(jnp.where(eye_b, g_b, 0.0), axis=0, keepdims=True)
        cum_row = jnp.sum(jnp.where(same & (ri <= ci), g_b, 0.0), axis=0, keepdims=True)
        cum_col = jnp.sum(jnp.where(eye_b, cum_row, 0.0), axis=1, keepdims=True)
        g_tot = jnp.sum(jnp.where(same, g_row, 0.0), axis=1, keepdims=True)
        decay = jnp.exp(jnp.where(causal, cum_col - cum_row, -jnp.inf))
        pre.append((decay, jnp.exp(cum_col), jnp.exp(g_tot - cum_col), g_tot))
    kk = [_dot_nt(k, k) for _, k, _, _, _ in units]
    qk = [_dot_nt(q, k) for q, k, _, _, _ in units]
    low = [jnp.where(same & (ri > ci), un[3] * kk_u * pr[0], 0.0) for un, kk_u, pr in zip(units, kk, pre)]
    x = [eye - lw for lw in low]
    pw = low
    for _ in range(n_sq):
        pw = [jnp.dot(m, m, precision=HI, preferred_element_type=F32) for m in pw]
        x = [xi + jnp.dot(xi, m, precision=HI, preferred_element_type=F32) for xi, m in zip(x, pw)]
    u = [_dot(t, un[2] * un[3][:, 0:1]) for t, un in zip(x, units)]
    w = [_dot(t, un[1] * (un[3][:, 0:1] * pr[1])) for t, un, pr in zip(x, units, pre)]
    return [(u_u, w_u, qk_u * pr[0], un[0] * pr[1], un[1] * pr[2], pr[3])
            for u_u, w_u, qk_u, un, pr in zip(u, w, qk, units, pre)]


def _delta_long_kernel(q_ref, k_ref, v_ref, g_ref, s0_ref, o_ref, s_ref, *, chunk, n_chunks, n_sq):
    t = pl.program_id(1)

    @pl.when(t == 0)
    def _():
        s_ref[...] = s0_ref[...]

    same = jnp.full((chunk, chunk), True)

    def body(c, carry):
        rows = pl.ds(pl.multiple_of(c * chunk, chunk), chunk)
        gates = g_ref[rows, :]
        heads = range(HEADS)
        hcols = [slice(hd * DK, (hd + 1) * DK) for hd in heads]
        terms = _chunk_terms(
            [(q_ref[rows, hs], k_ref[rows, hs], v_ref[rows, hs],
              jnp.broadcast_to(gates[:, hd:hd + 1], (chunk, chunk)),
              jnp.broadcast_to(gates[:, HEADS + hd:HEADS + hd + 1], (chunk, chunk)))
             for hd, hs in zip(heads, hcols)], same, n_sq)
        s = [s_ref[0, hd] for hd in heads]
        v_new = [tm[0] - _dot(tm[1], s_h) for tm, s_h in zip(terms, s)]
        qs = [_dot(tm[3], s_h) for tm, s_h in zip(terms, s)]
        av = [_dot(tm[2], vn) for tm, vn in zip(terms, v_new)]
        kv = [_dot_tn(tm[4], vn) for tm, vn in zip(terms, v_new)]
        for hd in heads:
            o_ref[rows, hcols[hd]] = qs[hd] + av[hd]
            s_ref[0, hd] = s[hd] * jnp.exp(terms[hd][5][0:1, :]) + kv[hd]
        return carry

    lax.fori_loop(0, n_chunks, body, 0)


def _delta_long(q, k, v, gates, s0, *, n, nb, nt, rows, lay, chunk):
    row_spec = functools.partial(lay.row_spec, rows)
    s_spec = pl.BlockSpec((1, HEADS, DK, DV), lambda b, t: (b, 0, 0, 0))
    n_sq = chunk.bit_length() - 2
    kern = functools.partial(_delta_long_kernel, chunk=chunk, n_chunks=rows // chunk, n_sq=n_sq)
    return pl.pallas_call(
        kern,
        grid=(nb, nt),
        in_specs=[row_spec(KEY), row_spec(KEY), row_spec(VAL), row_spec(LANE), s_spec],
        out_specs=[row_spec(VAL), s_spec],
        out_shape=[jax.ShapeDtypeStruct((n, VAL), F32), jax.ShapeDtypeStruct(s0.shape, F32)],
        compiler_params=_PARAMS_2D,
        name="gdn_delta_long",
    )(q, k, v, gates, s0)


UNITS = LANE
TILE_CHUNKS = UNITS // HEADS
TILE_ROWS = TILE_CHUNKS * GDN_CHUNK
PACK_STRIDE = UNITS + SUB


def _col(x, lane, width):
    return jnp.broadcast_to(x[:, lane:lane + 1], (x.shape[0], width))


def _delta_tile_kernel(q_ref, k_ref, v_ref, g_ref, s0_ref, o_ref, s_ref,
                       low_scr, lowt_scr, tcol_scr, tinv_scr, u_scr, kg_scr, w_scr, qd_scr, kd_scr, a_scr, gt_scr,
                       beta_scr):
    @pl.when(pl.program_id(1) == 0)
    def _():
        s_ref[...] = s0_ref[...]

    chunk = GDN_CHUNK
    half = chunk // 2
    row_id = lax.broadcasted_iota(jnp.int32, (chunk, LANE), 0)
    lane_id = lax.broadcasted_iota(jnp.int32, (chunk, LANE), 1)
    col_id = lane_id % chunk
    first_half = lax.broadcasted_iota(jnp.int32, (half, LANE), 1) < chunk
    heads = range(HEADS)
    hcols = [slice(h * DK, (h + 1) * DK) for h in heads]

    def phase_1a(c, carry):
        rows = pl.ds(pl.multiple_of(c * chunk, chunk), chunk)
        gates = g_ref[rows, :]
        cum = gates
        d = 1
        while d < chunk:
            cum = cum + jnp.where(row_id >= d, pltpu.roll(cum, d, 0), 0.0)
            d *= 2
        tot = cum[chunk - 1:chunk, :]
        cum_t = jnp.concatenate([cum, cum], axis=0).T
        beta_scr[...] = jnp.where(lane_id // HEADS == c, pltpu.roll(gates, c * HEADS, 1), beta_scr[...])
        def scores(h):
            qh, kh = q_ref[rows, hcols[h]], k_ref[rows, hcols[h]]
            return qh, kh, _dot_nt(jnp.concatenate([qh, kh], axis=0), jnp.concatenate([kh, kh], axis=0))

        ahead = scores(0)
        for h in heads:
            qh, kh, sc = ahead
            if h + 1 < HEADS:
                ahead = scores(h + 1)
            cum_b = _col(cum, HEADS + h, DK)
            tot_b = jnp.broadcast_to(tot[:, HEADS + h:HEADS + h + 1], (chunk, DK))
            diff = cum_b - jnp.broadcast_to(cum_t[HEADS + h:HEADS + h + 1, :], (chunk, LANE))
            decay = jnp.exp(jnp.where(row_id >= col_id, diff, -jnp.inf))
            unit = c * HEADS + h
            low = jnp.where(row_id > col_id, sc[chunk:] * decay, 0.0)
            low_scr[pl.ds(unit, half, stride=PACK_STRIDE), :] = jnp.where(first_half, low[:half], low[half:])
            a_scr[unit] = (sc[:chunk] * decay)[:, :chunk].astype(BF16)
            e1 = jnp.exp(cum_b)
            kf = kh.astype(F32)
            kg_scr[rows, hcols[h]] = (kf * e1).astype(BF16)
            qd_scr[rows, hcols[h]] = (qh.astype(F32) * e1).astype(BF16)
            kd_scr[rows, hcols[h]] = (kf * jnp.exp(tot_b - cum_b)).astype(BF16)
        e_tot = jnp.exp(tot)
        gt_scr[pl.ds(pl.multiple_of(c * HEADS, HEADS), HEADS), :] = jnp.concatenate(
            [jnp.broadcast_to(e_tot[:, HEADS + h:HEADS + h + 1], (1, LANE)) for h in heads], axis=0)
        return carry

    beta_scr[...] = jnp.zeros(beta_scr.shape, F32)
    lax.fori_loop(0, TILE_CHUNKS, phase_1a, 0)

    for i2 in range(half):
        slab = low_scr[pl.ds(i2 * PACK_STRIDE, UNITS), :].T
        lowt_scr[i2] = slab[:chunk]
        lowt_scr[i2 + half] = slab[chunk:]
    sub = lax.broadcasted_iota(jnp.int32, (SUB, LANE), 0)
    zero = jnp.zeros((SUB, LANE), F32)
    for i in range(chunk):
        nblk = i // SUB + 1
        acc = [[zero, zero] for _ in range(nblk)]
        if i:
            beta_i = jnp.broadcast_to(beta_scr[i:i + 1, :], (SUB, LANE))
        for j in range(i):
            lrow = jnp.broadcast_to(lowt_scr[i, j:j + 1, :], (SUB, LANE))
            for cb in range(j // SUB + 1):
                acc[cb][j % 2] = acc[cb][j % 2] + lrow * tcol_scr[j, cb * SUB:(cb + 1) * SUB, :]
        for cb in range(chunk // SUB):
            blk = zero
            if i and cb < nblk:
                blk = -(beta_i * (acc[cb][0] + acc[cb][1]))
            if cb == i // SUB:
                blk = blk + (sub == i % SUB).astype(F32)
            tcol_scr[i, cb * SUB:(cb + 1) * SUB, :] = blk
    for i2 in range(half):
        tinv_scr[pl.ds(i2 * PACK_STRIDE, UNITS), :] = jnp.concatenate([tcol_scr[i2], tcol_scr[i2 + half]], axis=0).T

    def uw_compute(c):
        rows = pl.ds(pl.multiple_of(c * chunk, chunk), chunk)
        gates = g_ref[rows, :]
        gates_t = jnp.concatenate([gates, gates], axis=0).T
        out = []
        lhs = []
        for h in heads:
            packed = tinv_scr[pl.ds(c * HEADS + h, half, stride=PACK_STRIDE), :]
            packed = packed * jnp.broadcast_to(gates_t[h:h + 1, :], (half, LANE))
            lhs.append(jnp.concatenate([jnp.where(first_half, packed, 0.0),
                                        jnp.where(first_half, 0.0, packed)], axis=0))
        for h in heads:
            vk = jnp.concatenate([v_ref[rows, hcols[h]], kg_scr[rows, hcols[h]]], axis=1)
            out.append(_dot(lhs[h], jnp.concatenate([vk, vk], axis=0)))
        return out

    def uw_store(c, uw):
        rows = pl.ds(pl.multiple_of(c * chunk, chunk), chunk)
        for h in heads:
            u_scr[rows, hcols[h]] = uw[h][:, :DV]
            w_scr[rows, hcols[h]] = uw[h][:, DV:].astype(BF16)

    uw_store(0, uw_compute(0))

    def phase_2(c, carry):
        nxt = jnp.minimum(c + 1, TILE_CHUNKS - 1)
        rows = pl.ds(pl.multiple_of(c * chunk, chunk), chunk)
        s = [s_ref[0, h] for h in heads]
        m1 = [_dot(jnp.concatenate([w_scr[rows, hcols[h]], qd_scr[rows, hcols[h]]], axis=0), s[h].astype(BF16))
              for h in heads]
        uw = uw_compute(nxt)
        vb = [(u_scr[rows, hcols[h]] - m1[h][:chunk]).astype(BF16) for h in heads]
        av = [_dot(a_scr[c * HEADS + h], vb[h]) for h in heads]
        kv = [_dot_tn(kd_scr[rows, hcols[h]], vb[h]) for h in heads]
        for h in heads:
            o_ref[rows, hcols[h]] = m1[h][chunk:] + av[h]
            s_ref[0, h] = s[h] * gt_scr[pl.ds(c * HEADS + h, 1), :] + kv[h]
        uw_store(nxt, uw)
        return carry

    lax.fori_loop(0, TILE_CHUNKS, phase_2, 0)


def _delta_tiles(q, k, v, gates, s0, *, nb, nt):
    n = q.shape[0]
    row_spec = lambda c: pl.BlockSpec((TILE_ROWS, c), lambda b, t: (b * nt + t, 0))
    s_spec = pl.BlockSpec((1, HEADS, DK, DV), lambda b, t: (b, 0, 0, 0))
    s0_spec = s_spec if s0.shape[0] == nb else pl.BlockSpec((1, HEADS, DK, DV), lambda b, t: (0, 0, 0, 0))
    packed = pltpu.VMEM((GDN_CHUNK // 2 * PACK_STRIDE, LANE), F32)
    by_row = pltpu.VMEM((GDN_CHUNK, GDN_CHUNK, UNITS), F32)
    tile_bf16 = pltpu.VMEM((TILE_ROWS, KEY), BF16)
    return pl.pallas_call(
        _delta_tile_kernel,
        grid=(nb, nt),
        in_specs=[row_spec(KEY), row_spec(KEY), row_spec(VAL), row_spec(LANE), s0_spec],
        out_specs=[row_spec(VAL), s_spec],
        out_shape=[jax.ShapeDtypeStruct((n, VAL), F32), jax.ShapeDtypeStruct((nb, HEADS, DK, DV), F32)],
        scratch_shapes=[packed, by_row, by_row, packed,
                        pltpu.VMEM((TILE_ROWS, VAL), F32), tile_bf16, tile_bf16, tile_bf16, tile_bf16,
                        pltpu.VMEM((UNITS, GDN_CHUNK, GDN_CHUNK), BF16), pltpu.VMEM((UNITS, LANE), F32),
                        pltpu.VMEM((GDN_CHUNK, UNITS), F32)],
        compiler_params=_PARAMS_2D,
        name="gdn_delta_tiles",
    )(q, k, v, gates, s0)


def _delta_short_kernel(q_ref, k_ref, v_ref, g_ref, s0_ref, o_ref, s_ref, *, nseq):
    c = HEADS * SUB
    ri = lax.broadcasted_iota(jnp.int32, (c, c), 0)
    ci = lax.broadcasted_iota(jnp.int32, (c, c), 1)
    same = (ri // SUB) == (ci // SUB)
    rh = lax.broadcasted_iota(jnp.int32, (2 * c, KEY), 0) % c // SUB
    ch = lax.broadcasted_iota(jnp.int32, (2 * c, KEY), 1) // DK
    head_cols2 = rh == ch
    head_cols = head_cols2[:c]
    heads = range(HEADS)
    seqs = range(nseq)
    rows = [slice(sq * SUB, (sq + 1) * SUB) for sq in seqs]

    def unit(sq):
        gates = g_ref[rows[sq], :]
        heads_to_rows = lambda ref: jnp.concatenate(
            [ref[rows[sq], hd * DK:(hd + 1) * DK] for hd in heads], axis=0)
        gate_rows = lambda lane0: jnp.concatenate(
            [jnp.broadcast_to(gates[:, lane0 + hd:lane0 + hd + 1], (SUB, c)) for hd in heads], axis=0)
        return heads_to_rows(q_ref), heads_to_rows(k_ref), heads_to_rows(v_ref), gate_rows(0), gate_rows(HEADS)

    terms = _chunk_terms([unit(sq) for sq in seqs], same, 2)
    spread = lambda m: jnp.where(head_cols if m.shape[0] == c else head_cols2,
                                 jnp.concatenate([m] * HEADS, axis=1), 0.0)
    s = [s0_ref[sq].reshape(KEY, DV) for sq in seqs]
    ws = [_dot(spread(jnp.concatenate([tm[1], tm[3]], axis=0)), s_q) for tm, s_q in zip(terms, s)]
    v_new = [tm[0] - m[:c] for tm, m in zip(terms, ws)]
    qs = [m[c:] for m in ws]
    av = [_dot(tm[2], vn) for tm, vn in zip(terms, v_new)]
    upd = [_dot_tn(spread(tm[4]), vn) for tm, vn in zip(terms, v_new)]
    for sq in seqs:
        o = qs[sq] + av[sq]
        o_ref[rows[sq], :] = jnp.concatenate([o[hd * SUB:(hd + 1) * SUB] for hd in heads], axis=1)
        for hd in heads:
            ks = slice(hd * DK, (hd + 1) * DK)
            s_ref[sq, hd] = s[sq][ks] * jnp.exp(terms[sq][5][hd * SUB:hd * SUB + 1, :]) + upd[sq][ks]


def _delta_short(q, k, v, gates, s0, *, nseq):
    n = q.shape[0]
    nb = n // (nseq * SUB)
    row_spec = lambda c: pl.BlockSpec((nseq * SUB, c), lambda b: (b, 0))
    s_spec = pl.BlockSpec((nseq, HEADS, DK, DV), lambda b: (b, 0, 0, 0))
    return pl.pallas_call(
        functools.partial(_delta_short_kernel, nseq=nseq),
        grid=(nb,),
        in_specs=[row_spec(KEY), row_spec(KEY), row_spec(VAL), row_spec(LANE), s_spec],
        out_specs=[row_spec(VAL), s_spec],
        out_shape=[jax.ShapeDtypeStruct((n, VAL), F32), jax.ShapeDtypeStruct(s0.shape, F32)],
        compiler_params=pltpu.CompilerParams(dimension_semantics=("arbitrary",),
                                             vmem_limit_bytes=VMEM_LIMIT),
        name="gdn_delta_short",
    )(q, k, v, gates, s0)


def _conv_ffn(h, lay, prev_ref, carry_ref, new_ref, wup_ref, cw_ref, cb_ref, wdown_ref, act_ref):
    lay.init_carry(carry_ref, prev_ref)
    n_chunks = D_FF // FFN_COLS
    col_pair = lambda c: [slice(part * D_FF + c * FFN_COLS, part * D_FF + (c + 1) * FFN_COLS) for part in range(2)]
    up = lambda c: [_dot(h, wup_ref[:, cols]) for cols in col_pair(c)]
    ahead = up(0)
    for c in range(n_chunks):
        us = ahead
        if c + 1 < n_chunks:
            ahead = up(c + 1)
        a, b = [lay.conv(u, cw_ref, prev_ref, carry_ref, new_ref, cols, FFN_CONV_W) + cb_ref[:, cols]
                for u, cols in zip(us, col_pair(c))]
        act_ref[:, c * FFN_COLS:(c + 1) * FFN_COLS] = (_silu(a) * b).astype(BF16)
    return _dot(act_ref[...], wdown_ref[...])


def _mixffn_kernel(x_ref, o_ref, z_ref, gnw_ref, wout_ref, nf_ref, prev_ref, wup_ref, cw_ref, cb_ref, wdown_ref,
                   y_ref, new_ref, carry_ref, act_ref, gated_ref, *, lay):
    for hd in range(HEADS):
        hs = slice(hd * DV, (hd + 1) * DV)
        o = lay.load(o_ref, hs)
        o = o * lax.rsqrt(jnp.mean(o * o, axis=-1, keepdims=True) + EPS) * gnw_ref[...]
        gated_ref[:, hs] = (o * _silu(lay.load(z_ref, hs))).astype(BF16)
    x1 = lay.load(x_ref) + _dot(gated_ref[...], wout_ref[...])
    h = _rms(x1, nf_ref[...]).astype(BF16)
    f = _conv_ffn(h, lay, prev_ref, carry_ref, new_ref, wup_ref, cw_ref, cb_ref, wdown_ref, act_ref)
    lay.store(y_ref, x1 + f)


def _ffn_specs(lay, prev, nb, p, layer):
    return [lay.state_spec(prev, nb)] + [_layer_spec(a, layer) for a in _ffn_args(p)]


def _ffn_args(p):
    return [p["w_up"], p["ffn_conv_w"], p["ffn_conv_b"], p["w_down"]]


def _mixffn(x, o, z, prev, p, *, n, nb, nt, rows, lay):
    row_spec = functools.partial(lay.row_spec, rows)
    return pl.pallas_call(
        functools.partial(_mixffn_kernel, lay=lay),
        grid=(nb, nt),
        in_specs=[row_spec(D_MODEL), row_spec(VAL), row_spec(VAL), _const_spec((1, DV)),
                  _layer_spec(p["w_out"], 0), _layer_spec(p["norm_ffn"], 0)] + _ffn_specs(lay, prev, nb, p, 0),
        out_specs=[row_spec(D_MODEL), lay.new_state(prev, nb)[0]],
        out_shape=[lay.row_shape(n, D_MODEL), lay.new_state(prev, nb)[1]],
        scratch_shapes=[pltpu.VMEM((SUB, 2 * D_FF), F32), pltpu.VMEM((rows, D_FF), BF16),
                        pltpu.VMEM((rows, VAL), BF16)],
        compiler_params=_PARAMS_2D,
        name="gdn_out_ffn",
    )(x, o, z, p["gdn_norm_w"], p["w_out"], p["norm_ffn"], prev, *_ffn_args(p))


def _window_sum(hcat, level, step):
    s = hcat
    for lv in range(level):
        d = step << lv
        s = s[d:] + s[:s.shape[0] - d]
    return s


def _poolffn_kernel(x_ref, pprev_ref, nm_ref, pw_ref, ps_ref, nf_ref, prev_ref, wup_ref, cw_ref, cb_ref, wdown_ref,
                    nfin_ref, y_ref, pnew_ref, new_ref, pcarry_ref, carry_ref, act_ref, mix_ref, *, lay, start_pos):
    x = lay.load(x_ref)
    rows = x.shape[0]
    h = _rms(x, nm_ref[...])
    row = lax.broadcasted_iota(jnp.int32, (rows, 1), 0)
    if lay.nseq:
        step = lay.nseq
        pos = start_pos + row // step
        hcat = jnp.concatenate([lay.load_state(pprev_ref), h], axis=0)
        lay.store_state(pnew_ref, hcat[rows:])
        first = POOL_BUF * step
    else:
        step = 1
        pos = start_pos + pl.program_id(1) * rows + row
        if lay.nt > 1:
            @pl.when(pl.program_id(1) == 0)
            def _():
                pcarry_ref[...] = pprev_ref[...]
            pprev = pcarry_ref[...]
        else:
            pprev = pprev_ref[...]
        hcat = jnp.concatenate([pprev, h], axis=0)
        pnew_ref[...] = hcat[rows:]
        if lay.nt > 1:
            pcarry_ref[...] = hcat[rows:]
        first = 2 * SUB
    for gi, win in enumerate(POOL_WINDOWS):
        cols = slice(gi * POOL_GW, (gi + 1) * POOL_GW)
        if lay.nseq:
            lo = first - (win - 1) * step
            wsum = _window_sum(hcat[:, cols], gi + 1, step)[lo:lo + rows]
        else:
            s = hcat[:, cols]
            for lv in range(gi + 1):
                s = s + pltpu.roll(s, 1 << lv, 0)
            wsum = s[first:]
        cnt = jnp.minimum(win, pos + 1).astype(F32)
        pooled = wsum / cnt - h[:, cols]
        mix_ref[:, cols] = _dot(pooled.astype(BF16), pw_ref[gi])
    x3 = x + mix_ref[...] * ps_ref[...]
    hf = _rms(x3, nf_ref[...]).astype(BF16)
    f = _conv_ffn(hf, lay, prev_ref, carry_ref, new_ref, wup_ref, cw_ref, cb_ref, wdown_ref, act_ref)
    lay.store(y_ref, _rms(x3 + f, nfin_ref[...]))


def _poolffn(x, pool_prev, prev, p, *, n, nb, nt, rows, lay, start_pos):
    row_spec = functools.partial(lay.row_spec, rows)
    return pl.pallas_call(
        functools.partial(_poolffn_kernel, lay=lay, start_pos=start_pos),
        grid=(nb, nt),
        in_specs=[row_spec(D_MODEL), lay.state_spec(pool_prev, nb),
                  _layer_spec(p["norm_mix"], 1), _layer_spec(p["pool_w"], 0), _const_spec((1, D_MODEL)),
                  _layer_spec(p["norm_ffn"], 1)] + _ffn_specs(lay, prev, nb, p, 1) + [_const_spec((1, D_MODEL))],
        out_specs=[row_spec(D_MODEL), lay.new_state(pool_prev, nb)[0], lay.new_state(prev, nb)[0]],
        out_shape=[lay.row_shape(n, D_MODEL), lay.new_state(pool_prev, nb)[1], lay.new_state(prev, nb)[1]],
        scratch_shapes=[pltpu.VMEM((2 * SUB, D_MODEL), F32), pltpu.VMEM((SUB, 2 * D_FF), F32),
                        pltpu.VMEM((rows, D_FF), BF16), pltpu.VMEM((rows, D_MODEL), F32)],
        compiler_params=_PARAMS_2D,
        name="pool_ffn",
    )(x, pool_prev, p["norm_mix"], p["pool_w"], p["pool_scale"], p["norm_ffn"], prev, *_ffn_args(p),
      p["norm_final"])


def _trunk(x, n_seq, seq_len, conv_prev, s0, pool_prev, ffn_prev, start_pos, p, shared_state=False):
    n = n_seq * seq_len
    if seq_len == SUB:
        rows = min(SHORT_ROW_TILE, n)
        geom = dict(n=n, nb=n // rows, nt=1, rows=rows, lay=_Layout(rows // SUB, 1))
    else:
        rows = min(ROW_TILE, seq_len)
        geom = dict(n=n, nb=n_seq, nt=seq_len // rows, rows=rows, lay=_Layout(0, seq_len // rows, shared_state))
    tiled = seq_len % TILE_ROWS == 0
    q, k, v, z, gates, conv_new = _proj(x, conv_prev, p, BF16 if tiled else F32, **geom)
    if seq_len == SUB:
        flat = lambda a: a.reshape(n, a.shape[-1])
        o, s_new = _delta_short(flat(q), flat(k), flat(v), flat(gates), s0, nseq=min(SHORT_SEQS, n_seq))
        o = o.reshape(n_seq, SUB, VAL)
    elif tiled:
        o, s_new = _delta_tiles(q, k, v, gates, s0, nb=n_seq, nt=seq_len // TILE_ROWS)
    else:
        o, s_new = _delta_long(q, k, v, gates, s0, chunk=min(GDN_CHUNK, seq_len), **geom)
    x2, ffn_new0 = _mixffn(x, o, z, ffn_prev[0], p, **geom)
    y, pool_new, ffn_new1 = _poolffn(x2, pool_prev, ffn_prev[1], p, start_pos=start_pos, **geom)
    return y, conv_new, s_new, pool_new, (ffn_new0, ffn_new1)


def _prep_params(norm_mix, norm_ffn, gdn_w_in, gdn_conv_w, gdn_A_log, gdn_dt_bias, gdn_norm_w, gdn_w_out,
                 pool_w, pool_scale, ffn_w_up, ffn_conv_w, ffn_conv_b, ffn_w_down, norm_final):
    gate_pad = ((0, 0), (0, LANE - 2 * HEADS))
    head_pad = ((0, 0), (HEADS, LANE - 2 * HEADS))
    row = lambda a: a.reshape(a.shape[0], 1, a.shape[1])
    return {
        "norm_mix": row(norm_mix), "norm_ffn": row(norm_ffn),
        "w_in": gdn_w_in.astype(BF16),
        "w_g": jnp.pad(gdn_w_in[0][:, QKV + VAL:], gate_pad).astype(BF16),
        "gdn_conv_w": gdn_conv_w,
        "a_log": jnp.pad(gdn_A_log[0:1], head_pad),
        "dt_bias": jnp.pad(gdn_dt_bias[0:1], head_pad),
        "gdn_norm_w": gdn_norm_w[0:1],
        "w_out": gdn_w_out.astype(BF16),
        "pool_w": pool_w.astype(BF16),
        "pool_scale": pool_scale[0:1],
        "w_up": ffn_w_up.astype(BF16), "ffn_conv_w": ffn_conv_w, "ffn_conv_b": row(ffn_conv_b),
        "w_down": ffn_w_down.astype(BF16),
        "norm_final": norm_final[None],
    }


def kernel(x_prompt, x_sample, state_gdn_conv, state_gdn_rec, state_pool, state_ffn_conv, meta_tokens, norm_mix, norm_ffn, gdn_w_in, gdn_conv_w, gdn_A_log, gdn_dt_bias, gdn_norm_w, gdn_w_out, pool_w, pool_scale, ffn_w_up, ffn_conv_w, ffn_conv_b, ffn_w_down, norm_final):
    p = _prep_params(norm_mix, norm_ffn, gdn_w_in, gdn_conv_w, gdn_A_log, gdn_dt_bias, gdn_norm_w, gdn_w_out,
                     pool_w, pool_scale, ffn_w_up, ffn_conv_w, ffn_conv_b, ffn_w_down, norm_final)
    bp, lp, _ = x_prompt.shape
    bs, ls, _ = x_sample.shape

    _, m_conv, m_s, m_pool, m_ffn = _trunk(
        meta_tokens.astype(F32), 1, N_META,
        jnp.zeros((SUB, QKV), F32), jnp.zeros((1, HEADS, DK, DV), F32), jnp.zeros((2 * SUB, D_MODEL), F32),
        (jnp.zeros((SUB, 2 * D_FF), F32), jnp.zeros((SUB, 2 * D_FF), F32)), 0, p)

    yp, p_conv, p_s, p_pool, p_ffn = _trunk(
        x_prompt.reshape(bp * lp, D_MODEL), bp, lp, m_conv, m_s, m_pool, m_ffn, N_META, p, shared_state=True)

    tm = lambda a: jnp.swapaxes(a, 0, 1)
    ys, s_conv, s_s, s_pool, s_ffn = _trunk(
        x_sample, bs, ls,
        tm(state_gdn_conv[0]), state_gdn_rec[0], tm(state_pool[0]),
        (tm(state_ffn_conv[0]), tm(state_ffn_conv[1])), PAST_LEN, p)

    last = lambda a, b, r: a.reshape(b, -1, a.shape[-1])[:, -r:]
    return (yp.reshape(bp, lp, D_MODEL), ys,
            last(p_conv, bp, CONV_W - 1)[None], p_s[None], last(p_pool, bp, POOL_BUF)[None],
            jnp.stack([last(p_ffn[0], bp, FFN_CONV_W - 1), last(p_ffn[1], bp, FFN_CONV_W - 1)]),
            tm(s_conv)[None], s_s[None], tm(s_pool)[None],
            jnp.stack([tm(s_ffn[0]), tm(s_ffn[1])]))
```

```python
import functools

import jax
import jax.numpy as jnp
from jax import lax
from jax.experimental import pallas as pl
from jax.experimental.pallas import tpu as pltpu

D_MODEL = 1024
N_META = 16
EPS = 1e-6
HEADS = 8
DK = 128
DV = 128
KEY = HEADS * DK
VAL = HEADS * DV
QKV = 2 * KEY + VAL
CONV_W = 4
GDN_CHUNK = 64
POOL_WINDOWS = (2, 4, 8, 16)
POOL_GW = D_MODEL // 4
POOL_BUF = 15
D_FF = 2816
FFN_CONV_W = 3
PAST_LEN = 16384

SUB = 8
LANE = 128
FFN_COLS = 256
PROJ_COLS = 256
ROW_TILE = 512
SHORT_ROW_TILE = 256
SHORT_SEQS = 8
VMEM_LIMIT = 56 * 1024 * 1024

F32 = jnp.float32
BF16 = jnp.bfloat16
HI = lax.Precision.HIGHEST


def _dot(a, b):
    return jnp.dot(a.astype(BF16), b.astype(BF16), preferred_element_type=F32)


def _dot_nt(a, b):
    return lax.dot_general(a.astype(BF16), b.astype(BF16), (((1,), (1,)), ((), ())), preferred_element_type=F32)


def _dot_tn(a, b):
    return lax.dot_general(a.astype(BF16), b.astype(BF16), (((0,), (0,)), ((), ())), preferred_element_type=F32)


def _rms(x, w):
    return x * lax.rsqrt(jnp.mean(x * x, axis=-1, keepdims=True) + EPS) * w


def _silu(x):
    return x * jax.nn.sigmoid(x)


def _softplus(x):
    return jnp.maximum(x, 0.0) + jnp.log1p(jnp.exp(-jnp.abs(x)))


def _conv_taps(u, prev, width, nseq):
    rows = u.shape[0]
    xcat = jnp.concatenate([prev, u], axis=0)
    if nseq:
        return [xcat[j * nseq:j * nseq + rows] for j in range(width)], xcat[rows:]
    taps = [pltpu.roll(xcat, width - 1 - j, 0)[SUB:] for j in range(width - 1)] + [u]
    return taps, u[rows - SUB:]


class _Layout:
    def __init__(self, nseq, nt, shared=False):
        self.nseq = nseq
        self.nt = nt
        self.shared = shared

    def load(self, ref, cols=slice(None)):
        if self.nseq:
            return jnp.concatenate([ref[:, t, cols] for t in range(SUB)], axis=0)
        return ref[:, cols]

    def store(self, ref, val, cols=slice(None)):
        val = val.astype(ref.dtype)
        if self.nseq:
            for t in range(SUB):
                ref[:, t, cols] = val[t * self.nseq:(t + 1) * self.nseq]
        else:
            ref[:, cols] = val

    def load_state(self, ref, cols=slice(None)):
        return jnp.concatenate([ref[i, :, cols] for i in range(ref.shape[0])], axis=0)

    def store_state(self, ref, val, cols=slice(None)):
        for i in range(ref.shape[0]):
            ref[i, :, cols] = val[i * self.nseq:(i + 1) * self.nseq]

    def init_carry(self, carry_ref, prev_ref):
        if not self.nseq and self.nt > 1:
            @pl.when(pl.program_id(1) == 0)
            def _():
                carry_ref[...] = prev_ref[...]

    def conv(self, u, w_ref, prev_ref, carry_ref, new_ref, cols, width):
        if self.nseq:
            prev = self.load_state(prev_ref, cols)
        else:
            prev = (carry_ref if self.nt > 1 else prev_ref)[:, cols]
        taps, new = _conv_taps(u, prev, width, self.nseq)
        out = w_ref[0:1, cols] * taps[0]
        for j in range(1, width):
            out = out + w_ref[j:j + 1, cols] * taps[j]
        if self.nseq:
            self.store_state(new_ref, new, cols)
        else:
            new_ref[:, cols] = new
            if self.nt > 1:
                carry_ref[:, cols] = new
        return out

    def row_spec(self, rows, c):
        if self.nseq:
            return pl.BlockSpec((self.nseq, SUB, c), lambda b, t: (b, 0, 0))
        nt = self.nt
        return pl.BlockSpec((rows, c), lambda b, t: (b * nt + t, 0))

    def row_shape(self, n, c, dtype=F32):
        return jax.ShapeDtypeStruct((n // SUB, SUB, c) if self.nseq else (n, c), dtype)

    def state_spec(self, arr, nb):
        if self.nseq:
            return pl.BlockSpec((arr.shape[0], self.nseq, arr.shape[2]), lambda b, t: (0, b, 0))
        if self.shared:
            return pl.BlockSpec(arr.shape, lambda b, t: (0, 0))
        return pl.BlockSpec((arr.shape[0] // nb, arr.shape[1]), lambda b, t: (b, 0))

    def new_state(self, arr, nb):
        if self.nseq:
            return self.state_spec(arr, nb), jax.ShapeDtypeStruct(arr.shape, F32)
        tile = arr.shape[0] if self.shared else arr.shape[0] // nb
        return (pl.BlockSpec((tile, arr.shape[1]), lambda b, t: (b, 0)),
                jax.ShapeDtypeStruct((tile * nb, arr.shape[1]), F32))


def _proj_kernel(x_ref, prev_ref, nw_ref, wqkv_ref, wg_ref, cw_ref, alog_ref, dtb_ref,
                 q_ref, k_ref, v_ref, z_ref, g_ref, cnew_ref, carry_ref, *, lay):
    h = _rms(lay.load(x_ref), nw_ref[...]).astype(BF16)
    lay.init_carry(carry_ref, prev_ref)
    blocks = []
    for c in range(QKV // PROJ_COLS):
        blocks.append(c * PROJ_COLS)
        if c % 3 == 2:
            blocks.append(QKV + (c // 3) * PROJ_COLS)

    def finish(lo, raw):
        cols = slice(lo, lo + PROJ_COLS)
        if lo >= QKV:
            lay.store(z_ref, raw, slice(lo - QKV, lo - QKV + PROJ_COLS))
            return
        part, off = divmod(lo, KEY)
        out_ref = (q_ref, k_ref, v_ref)[part]
        act = _silu(lay.conv(raw, cw_ref, prev_ref, carry_ref, cnew_ref, cols, CONV_W))
        if part == 2:
            lay.store(out_ref, act, slice(off, off + PROJ_COLS))
            return
        for hd in range(PROJ_COLS // DK):
            a = act[:, hd * DK:(hd + 1) * DK]
            n = a * lax.rsqrt(jnp.sum(a * a, axis=-1, keepdims=True) + EPS)
            lay.store(out_ref, n * (DK ** -0.5) if part == 0 else n, slice(off + hd * DK, off + (hd + 1) * DK))

    matmul = lambda lo: _dot(h, wqkv_ref[:, lo:lo + PROJ_COLS])
    ahead = matmul(blocks[0])
    for i, lo in enumerate(blocks):
        raw = ahead
        if i + 1 < len(blocks):
            ahead = matmul(blocks[i + 1])
        finish(lo, raw)
    lg = _dot(h, wg_ref[...])
    beta = jax.nn.sigmoid(lg)
    g = -jnp.exp(alog_ref[...]) * _softplus(lg + dtb_ref[...])
    lane = lax.broadcasted_iota(jnp.int32, lg.shape, 1)
    lay.store(g_ref, jnp.where(lane < HEADS, beta, g))


def _const_spec(shape):
    nd = len(shape)
    return pl.BlockSpec(shape, lambda *_: (0,) * nd, pipeline_mode=pl.Buffered(1))


def _layer_spec(arr, layer):
    tail = arr.shape[1:]
    return pl.BlockSpec((None,) + tail, lambda *_: (layer,) + (0,) * len(tail), pipeline_mode=pl.Buffered(1))


_PARAMS_2D = pltpu.CompilerParams(dimension_semantics=("arbitrary", "arbitrary"), vmem_limit_bytes=VMEM_LIMIT)


def _proj(x, prev, p, qkv_dtype, *, n, nb, nt, rows, lay):
    row_spec = functools.partial(lay.row_spec, rows)
    row_shape = functools.partial(lay.row_shape, n)
    qkv_shape = functools.partial(lay.row_shape, n, dtype=qkv_dtype)
    new_spec, new_shape = lay.new_state(prev, nb)
    return pl.pallas_call(
        functools.partial(_proj_kernel, lay=lay),
        grid=(nb, nt),
        in_specs=[row_spec(D_MODEL), lay.state_spec(prev, nb), _layer_spec(p["norm_mix"], 0),
                  _layer_spec(p["w_in"], 0), _const_spec((D_MODEL, LANE)),
                  _layer_spec(p["gdn_conv_w"], 0), _const_spec((1, LANE)), _const_spec((1, LANE))],
        out_specs=[row_spec(KEY), row_spec(KEY), row_spec(VAL), row_spec(VAL), row_spec(LANE), new_spec],
        out_shape=[qkv_shape(KEY), qkv_shape(KEY), qkv_shape(VAL), row_shape(VAL), row_shape(LANE), new_shape],
        scratch_shapes=[pltpu.VMEM((SUB, QKV), F32)],
        compiler_params=_PARAMS_2D,
        name="gdn_proj",
    )(x, prev, p["norm_mix"], p["w_in"], p["w_g"], p["gdn_conv_w"], p["a_log"], p["dt_bias"])


def _chunk_terms(units, same, n_sq):
    c = units[0][0].shape[0]
    ri = lax.broadcasted_iota(jnp.int32, (c, c), 0)
    ci = lax.broadcasted_iota(jnp.int32, (c, c), 1)
    eye_b = ri == ci
    eye = eye_b.astype(F32)
    causal = same & (ri >= ci)
    pre = []
    for q, k, v, beta_b, g_b in units:
        g_row = jnp.sum(jnp.where(eye_b, g_b, 0.0), axis=0, keepdims=True)
        cum_row = jnp.sum(jnp.where(same & (ri <= ci), g_b, 0.0), axis=0, keepdims=True)
        cum_col = jnp.sum(jnp.where(eye_b, cum_row, 0.0), axis=1, keepdims=True)
        g_tot = jnp.sum(jnp.where(same, g_row, 0.0), axis=1, keepdims=True)
        decay = jnp.exp(jnp.where(causal, cum_col - cum_row, -jnp.inf))
        pre.append((decay, jnp.exp(cum_col), jnp.exp(g_tot - cum_col), g_tot))
    kk = [_dot_nt(k, k) for _, k, _, _, _ in units]
    qk = [_dot_nt(q, k) for q, k, _, _, _ in units]
    low = [jnp.where(same & (ri > ci), un[3] * kk_u * pr[0], 0.0) for un, kk_u, pr in zip(units, kk, pre)]
    x = [eye - lw for lw in low]
    pw = low
    for _ in range(n_sq):
        pw = [jnp.dot(m, m, precision=HI, preferred_element_type=F32) for m in pw]
        x = [xi + jnp.dot(xi, m, precision=HI, preferred_element_type=F32) for xi, m in zip(x, pw)]
    u = [_dot(t, un[2] * un[3][:, 0:1]) for t, un in zip(x, units)]
    w = [_dot(t, un[1] * (un[3][:, 0:1] * pr[1])) for t, un, pr in zip(x, units, pre)]
    return [(u_u, w_u, qk_u * pr[0], un[0] * pr[1], un[1] * pr[2], pr[3])
            for u_u, w_u, qk_u, un, pr in zip(u, w, qk, units, pre)]


def _delta_long_kernel(q_ref, k_ref, v_ref, g_ref, s0_ref, o_ref, s_ref, *, chunk, n_chunks, n_sq):
    t = pl.program_id(1)

    @pl.when(t == 0)
    def _():
        s_ref[...] = s0_ref[...]

    same = jnp.full((chunk, chunk), True)

    def body(c, carry):
        rows = pl.ds(pl.multiple_of(c * chunk, chunk), chunk)
        gates = g_ref[rows, :]
        heads = range(HEADS)
        hcols = [slice(hd * DK, (hd + 1) * DK) for hd in heads]
        terms = _chunk_terms(
            [(q_ref[rows, hs], k_ref[rows, hs], v_ref[rows, hs],
              jnp.broadcast_to(gates[:, hd:hd + 1], (chunk, chunk)),
              jnp.broadcast_to(gates[:, HEADS + hd:HEADS + hd + 1], (chunk, chunk)))
             for hd, hs in zip(heads, hcols)], same, n_sq)
        s = [s_ref[0, hd] for hd in heads]
        v_new = [tm[0] - _dot(tm[1], s_h) for tm, s_h in zip(terms, s)]
        qs = [_dot(tm[3], s_h) for tm, s_h in zip(terms, s)]
        av = [_dot(tm[2], vn) for tm, vn in zip(terms, v_new)]
        kv = [_dot_tn(tm[4], vn) for tm, vn in zip(terms, v_new)]
        for hd in heads:
            o_ref[rows, hcols[hd]] = qs[hd] + av[hd]
            s_ref[0, hd] = s[hd] * jnp.exp(terms[hd][5][0:1, :]) + kv[hd]
        return carry

    lax.fori_loop(0, n_chunks, body, 0)


def _delta_long(q, k, v, gates, s0, *, n, nb, nt, rows, lay, chunk):
    row_spec = functools.partial(lay.row_spec, rows)
    s_spec = pl.BlockSpec((1, HEADS, DK, DV), lambda b, t: (b, 0, 0, 0))
    n_sq = chunk.bit_length() - 2
    kern = functools.partial(_delta_long_kernel, chunk=chunk, n_chunks=rows // chunk, n_sq=n_sq)
    return pl.pallas_call(
        kern,
        grid=(nb, nt),
        in_specs=[row_spec(KEY), row_spec(KEY), row_spec(VAL), row_spec(LANE), s_spec],
        out_specs=[row_spec(VAL), s_spec],
        out_shape=[jax.ShapeDtypeStruct((n, VAL), F32), jax.ShapeDtypeStruct(s0.shape, F32)],
        compiler_params=_PARAMS_2D,
        name="gdn_delta_long",
    )(q, k, v, gates, s0)


UNITS = LANE
TILE_CHUNKS = UNITS // HEADS
TILE_ROWS = TILE_CHUNKS * GDN_CHUNK
PAIR = 2
SEQ_CHUNKS = TILE_CHUNKS // PAIR
SEQ_ROWS = SEQ_CHUNKS * GDN_CHUNK
PACK_STRIDE = UNITS + SUB


def _col(x, lane, width):
    return jnp.broadcast_to(x[:, lane:lane + 1], (x.shape[0], width))


def _delta_tile_kernel(q_ref, k_ref, v_ref, g_ref, s0_ref, o_ref, s_ref,
                       low_scr, lowt_scr, tcol_scr, tinv_scr, u_scr, kg_scr, w_scr, qd_scr, kd_scr, a_scr, gt_scr,
                       beta_scr):
    @pl.when(pl.program_id(1) == 0)
    def _():
        for sq in range(PAIR):
            s_ref[sq] = s0_ref[min(sq, s0_ref.shape[0] - 1)]

    chunk = GDN_CHUNK

    def seq_rows(k):
        return k // SEQ_CHUNKS, pl.ds(pl.multiple_of(k % SEQ_CHUNKS * chunk, chunk), chunk)
    half = chunk // 2
    row_id = lax.broadcasted_iota(jnp.int32, (chunk, LANE), 0)
    lane_id = lax.broadcasted_iota(jnp.int32, (chunk, LANE), 1)
    col_id = lane_id % chunk
    first_half = lax.broadcasted_iota(jnp.int32, (half, LANE), 1) < chunk
    heads = range(HEADS)
    hcols = [slice(h * DK, (h + 1) * DK) for h in heads]

    def phase_1a(c, carry):
        rows = pl.ds(pl.multiple_of(c * chunk, chunk), chunk)
        sq, in_rows = seq_rows(c)
        gates = g_ref[sq, in_rows, :]
        cum = gates
        d = 1
        while d < chunk:
            cum = cum + jnp.where(row_id >= d, pltpu.roll(cum, d, 0), 0.0)
            d *= 2
        tot = cum[chunk - 1:chunk, :]
        cum_t = jnp.concatenate([cum, cum], axis=0).T
        beta_scr[...] = jnp.where(lane_id // HEADS == c, pltpu.roll(gates, c * HEADS, 1), beta_scr[...])
        def scores(h):
            qh, kh = q_ref[sq, in_rows, hcols[h]], k_ref[sq, in_rows, hcols[h]]
            return qh, kh, _dot_nt(jnp.concatenate([qh, kh], axis=0), jnp.concatenate([kh, kh], axis=0))

        ahead = scores(0)
        for h in heads:
            qh, kh, sc = ahead
            if h + 1 < HEADS:
                ahead = scores(h + 1)
            cum_b = _col(cum, HEADS + h, DK)
            tot_b = jnp.broadcast_to(tot[:, HEADS + h:HEADS + h + 1], (chunk, DK))
            diff = cum_b - jnp.broadcast_to(cum_t[HEADS + h:HEADS + h + 1, :], (chunk, LANE))
            decay = jnp.exp(jnp.where(row_id >= col_id, diff, -jnp.inf))
            unit = c * HEADS + h
            low = jnp.where(row_id > col_id, sc[chunk:] * decay, 0.0)
            low_scr[pl.ds(unit, half, stride=PACK_STRIDE), :] = jnp.where(first_half, low[:half], low[half:])
            a_scr[unit] = (sc[:chunk] * decay)[:, :chunk].astype(BF16)
            e1 = jnp.exp(cum_b)
            kf = kh.astype(F32)
            kg_scr[rows, hcols[h]] = (kf * e1).astype(BF16)
            qd_scr[rows, hcols[h]] = (qh.astype(F32) * e1).astype(BF16)
            kd_scr[rows, hcols[h]] = (kf * jnp.exp(tot_b - cum_b)).astype(BF16)
        e_tot = jnp.exp(tot)
        gt_scr[pl.ds(pl.multiple_of(c * HEADS, HEADS), HEADS), :] = jnp.concatenate(
            [jnp.broadcast_to(e_tot[:, HEADS + h:HEADS + h + 1], (1, LANE)) for h in heads], axis=0)
        return carry

    beta_scr[...] = jnp.zeros(beta_scr.shape, F32)
    lax.fori_loop(0, TILE_CHUNKS, phase_1a, 0)

    for i2 in range(half):
        slab = low_scr[pl.ds(i2 * PACK_STRIDE, UNITS), :].T
        lowt_scr[i2] = slab[:chunk]
        lowt_scr[i2 + half] = slab[chunk:]
    sub = lax.broadcasted_iota(jnp.int32, (SUB, LANE), 0)
    zero = jnp.zeros((SUB, LANE), F32)
    for i in range(chunk):
        nblk = i // SUB + 1
        acc = [[zero, zero] for _ in range(nblk)]
        if i:
            beta_i = jnp.broadcast_to(beta_scr[i:i + 1, :], (SUB, LANE))
        for j in range(i):
            lrow = jnp.broadcast_to(lowt_scr[i, j:j + 1, :], (SUB, LANE))
            for cb in range(j // SUB + 1):
                acc[cb][j % 2] = acc[cb][j % 2] + lrow * tcol_scr[j, cb * SUB:(cb + 1) * SUB, :]
        for cb in range(chunk // SUB):
            blk = zero
            if i and cb < nblk:
                blk = -(beta_i * (acc[cb][0] + acc[cb][1]))
            if cb == i // SUB:
                blk = blk + (sub == i % SUB).astype(F32)
            tcol_scr[i, cb * SUB:(cb + 1) * SUB, :] = blk
    for i2 in range(half):
        tinv_scr[pl.ds(i2 * PACK_STRIDE, UNITS), :] = jnp.concatenate([tcol_scr[i2], tcol_scr[i2 + half]], axis=0).T

    def uw_compute(c):
        rows = pl.ds(pl.multiple_of(c * chunk, chunk), chunk)
        sq, in_rows = seq_rows(c)
        gates = g_ref[sq, in_rows, :]
        gates_t = jnp.concatenate([gates, gates], axis=0).T
        out = []
        lhs = []
        for h in heads:
            packed = tinv_scr[pl.ds(c * HEADS + h, half, stride=PACK_STRIDE), :]
            packed = packed * jnp.broadcast_to(gates_t[h:h + 1, :], (half, LANE))
            lhs.append(jnp.concatenate([jnp.where(first_half, packed, 0.0),
                                        jnp.where(first_half, 0.0, packed)], axis=0))
        for h in heads:
            vk = jnp.concatenate([v_ref[sq, in_rows, hcols[h]], kg_scr[rows, hcols[h]]], axis=1)
            out.append(_dot(lhs[h], jnp.concatenate([vk, vk], axis=0)))
        return out

    def uw_store(c, uw):
        rows = pl.ds(pl.multiple_of(c * chunk, chunk), chunk)
        for h in heads:
            u_scr[rows, hcols[h]] = uw[h][:, :DV]
            w_scr[rows, hcols[h]] = uw[h][:, DV:].astype(BF16)

    pair = range(PAIR)
    for sq in pair:
        uw_store(sq * SEQ_CHUNKS, uw_compute(sq * SEQ_CHUNKS))

    def phase_2(c, carry):
        nxt = jnp.minimum(c + 1, SEQ_CHUNKS - 1)
        in_rows = pl.ds(pl.multiple_of(c * chunk, chunk), chunk)
        units = [(sq, h) for sq in pair for h in heads]
        slot = [sq * SEQ_CHUNKS + c for sq in pair]
        rows = [pl.ds(pl.multiple_of(k * chunk, chunk), chunk) for k in slot]
        s = [s_ref[sq, h] for sq, h in units]
        m1 = [_dot(jnp.concatenate([w_scr[rows[sq], hcols[h]], qd_scr[rows[sq], hcols[h]]], axis=0),
                   s_u.astype(BF16)) for (sq, h), s_u in zip(units, s)]
        uw = [uw_compute(sq * SEQ_CHUNKS + nxt) for sq in pair]
        vb = [(u_scr[rows[sq], hcols[h]] - m[:chunk]).astype(BF16) for (sq, h), m in zip(units, m1)]
        av = [_dot(a_scr[slot[sq] * HEADS + h], vb_u) for (sq, h), vb_u in zip(units, vb)]
        kv = [_dot_tn(kd_scr[rows[sq], hcols[h]], vb_u) for (sq, h), vb_u in zip(units, vb)]
        for (sq, h), s_u, m, av_u, kv_u in zip(units, s, m1, av, kv):
            o_ref[sq, in_rows, hcols[h]] = m[chunk:] + av_u
            s_ref[sq, h] = s_u * gt_scr[pl.ds(slot[sq] * HEADS + h, 1), :] + kv_u
        for sq in pair:
            uw_store(sq * SEQ_CHUNKS + nxt, uw[sq])
        return carry

    lax.fori_loop(0, SEQ_CHUNKS, phase_2, 0)


def _delta_tiles(q, k, v, gates, s0, *, n_seq, seq_len):
    n = q.shape[0]
    nb, nt = n_seq // PAIR, seq_len // SEQ_ROWS
    by_pair = lambda a: a.reshape(nb, PAIR, nt, SEQ_ROWS, a.shape[-1])
    row_spec = lambda c: pl.BlockSpec((None, PAIR, None, SEQ_ROWS, c), lambda b, t: (b, 0, t, 0, 0))
    s_spec = pl.BlockSpec((PAIR, HEADS, DK, DV), lambda b, t: (b, 0, 0, 0))
    s0_spec = s_spec if s0.shape[0] == n_seq else pl.BlockSpec((1, HEADS, DK, DV), lambda b, t: (0, 0, 0, 0))
    packed = pltpu.VMEM((GDN_CHUNK // 2 * PACK_STRIDE, LANE), F32)
    by_row = pltpu.VMEM((GDN_CHUNK, GDN_CHUNK, UNITS), F32)
    tile_bf16 = pltpu.VMEM((TILE_ROWS, KEY), BF16)
    o, s_new = pl.pallas_call(
        _delta_tile_kernel,
        grid=(nb, nt),
        in_specs=[row_spec(KEY), row_spec(KEY), row_spec(VAL), row_spec(LANE), s0_spec],
        out_specs=[row_spec(VAL), s_spec],
        out_shape=[jax.ShapeDtypeStruct((nb, PAIR, nt, SEQ_ROWS, VAL), F32),
                   jax.ShapeDtypeStruct((n_seq, HEADS, DK, DV), F32)],
        scratch_shapes=[packed, by_row, by_row, packed,
                        pltpu.VMEM((TILE_ROWS, VAL), F32), tile_bf16, tile_bf16, tile_bf16, tile_bf16,
                        pltpu.VMEM((UNITS, GDN_CHUNK, GDN_CHUNK), BF16), pltpu.VMEM((UNITS, LANE), F32),
                        pltpu.VMEM((GDN_CHUNK, UNITS), F32)],
        compiler_params=_PARAMS_2D,
        name="gdn_delta_tiles",
    )(by_pair(q), by_pair(k), by_pair(v), by_pair(gates), s0)
    return o.reshape(n, VAL), s_new


def _delta_short_kernel(q_ref, k_ref, v_ref, g_ref, s0_ref, o_ref, s_ref, *, nseq):
    c = HEADS * SUB
    ri = lax.broadcasted_iota(jnp.int32, (c, c), 0)
    ci = lax.broadcasted_iota(jnp.int32, (c, c), 1)
    same = (ri // SUB) == (ci // SUB)
    rh = lax.broadcasted_iota(jnp.int32, (2 * c, KEY), 0) % c // SUB
    ch = lax.broadcasted_iota(jnp.int32, (2 * c, KEY), 1) // DK
    head_cols2 = rh == ch
    head_cols = head_cols2[:c]
    heads = range(HEADS)
    seqs = range(nseq)
    rows = [slice(sq * SUB, (sq + 1) * SUB) for sq in seqs]

    def unit(sq):
        gates = g_ref[rows[sq], :]
        heads_to_rows = lambda ref: jnp.concatenate(
            [ref[rows[sq], hd * DK:(hd + 1) * DK] for hd in heads], axis=0)
        gate_rows = lambda lane0: jnp.concatenate(
            [jnp.broadcast_to(gates[:, lane0 + hd:lane0 + hd + 1], (SUB, c)) for hd in heads], axis=0)
        return heads_to_rows(q_ref), heads_to_rows(k_ref), heads_to_rows(v_ref), gate_rows(0), gate_rows(HEADS)

    terms = _chunk_terms([unit(sq) for sq in seqs], same, 2)
    spread = lambda m: jnp.where(head_cols if m.shape[0] == c else head_cols2,
                                 jnp.concatenate([m] * HEADS, axis=1), 0.0)
    s = [s0_ref[sq].reshape(KEY, DV) for sq in seqs]
    ws = [_dot(spread(jnp.concatenate([tm[1], tm[3]], axis=0)), s_q) for tm, s_q in zip(terms, s)]
    v_new = [tm[0] - m[:c] for tm, m in zip(terms, ws)]
    qs = [m[c:] for m in ws]
    av = [_dot(tm[2], vn) for tm, vn in zip(terms, v_new)]
    upd = [_dot_tn(spread(tm[4]), vn) for tm, vn in zip(terms, v_new)]
    for sq in seqs:
        o = qs[sq] + av[sq]
        o_ref[rows[sq], :] = jnp.concatenate([o[hd * SUB:(hd + 1) * SUB] for hd in heads], axis=1)
        for hd in heads:
            ks = slice(hd * DK, (hd + 1) * DK)
            s_ref[sq, hd] = s[sq][ks] * jnp.exp(terms[sq][5][hd * SUB:hd * SUB + 1, :]) + upd[sq][ks]


def _delta_short(q, k, v, gates, s0, *, nseq):
    n = q.shape[0]
    nb = n // (nseq * SUB)
    row_spec = lambda c: pl.BlockSpec((nseq * SUB, c), lambda b: (b, 0))
    s_spec = pl.BlockSpec((nseq, HEADS, DK, DV), lambda b: (b, 0, 0, 0))
    return pl.pallas_call(
        functools.partial(_delta_short_kernel, nseq=nseq),
        grid=(nb,),
        in_specs=[row_spec(KEY), row_spec(KEY), row_spec(VAL), row_spec(LANE), s_spec],
        out_specs=[row_spec(VAL), s_spec],
        out_shape=[jax.ShapeDtypeStruct((n, VAL), F32), jax.ShapeDtypeStruct(s0.shape, F32)],
        compiler_params=pltpu.CompilerParams(dimension_semantics=("arbitrary",),
                                             vmem_limit_bytes=VMEM_LIMIT),
        name="gdn_delta_short",
    )(q, k, v, gates, s0)


def _conv_ffn(h, lay, prev_ref, carry_ref, new_ref, wup_ref, cw_ref, cb_ref, wdown_ref, act_ref):
    lay.init_carry(carry_ref, prev_ref)
    n_chunks = D_FF // FFN_COLS
    col_pair = lambda c: [slice(part * D_FF + c * FFN_COLS, part * D_FF + (c + 1) * FFN_COLS) for part in range(2)]
    up = lambda c: [_dot(h, wup_ref[:, cols]) for cols in col_pair(c)]
    ahead = up(0)
    for c in range(n_chunks):
        us = ahead
        if c + 1 < n_chunks:
            ahead = up(c + 1)
        a, b = [lay.conv(u, cw_ref, prev_ref, carry_ref, new_ref, cols, FFN_CONV_W) + cb_ref[:, cols]
                for u, cols in zip(us, col_pair(c))]
        act_ref[:, c * FFN_COLS:(c + 1) * FFN_COLS] = (_silu(a) * b).astype(BF16)
    return _dot(act_ref[...], wdown_ref[...])


def _mixffn_kernel(x_ref, o_ref, z_ref, gnw_ref, wout_ref, nf_ref, prev_ref, wup_ref, cw_ref, cb_ref, wdown_ref,
                   y_ref, new_ref, carry_ref, act_ref, gated_ref, *, lay):
    for hd in range(HEADS):
        hs = slice(hd * DV, (hd + 1) * DV)
        o = lay.load(o_ref, hs)
        o = o * lax.rsqrt(jnp.mean(o * o, axis=-1, keepdims=True) + EPS) * gnw_ref[...]
        gated_ref[:, hs] = (o * _silu(lay.load(z_ref, hs))).astype(BF16)
    x1 = lay.load(x_ref) + _dot(gated_ref[...], wout_ref[...])
    h = _rms(x1, nf_ref[...]).astype(BF16)
    f = _conv_ffn(h, lay, prev_ref, carry_ref, new_ref, wup_ref, cw_ref, cb_ref, wdown_ref, act_ref)
    lay.store(y_ref, x1 + f)


def _ffn_specs(lay, prev, nb, p, layer):
    return [lay.state_spec(prev, nb)] + [_layer_spec(a, layer) for a in _ffn_args(p)]


def _ffn_args(p):
    return [p["w_up"], p["ffn_conv_w"], p["ffn_conv_b"], p["w_down"]]


def _mixffn(x, o, z, prev, p, *, n, nb, nt, rows, lay):
    row_spec = functools.partial(lay.row_spec, rows)
    return pl.pallas_call(
        functools.partial(_mixffn_kernel, lay=lay),
        grid=(nb, nt),
        in_specs=[row_spec(D_MODEL), row_spec(VAL), row_spec(VAL), _const_spec((1, DV)),
                  _layer_spec(p["w_out"], 0), _layer_spec(p["norm_ffn"], 0)] + _ffn_specs(lay, prev, nb, p, 0),
        out_specs=[row_spec(D_MODEL), lay.new_state(prev, nb)[0]],
        out_shape=[lay.row_shape(n, D_MODEL), lay.new_state(prev, nb)[1]],
        scratch_shapes=[pltpu.VMEM((SUB, 2 * D_FF), F32), pltpu.VMEM((rows, D_FF), BF16),
                        pltpu.VMEM((rows, VAL), BF16)],
        compiler_params=_PARAMS_2D,
        name="gdn_out_ffn",
    )(x, o, z, p["gdn_norm_w"], p["w_out"], p["norm_ffn"], prev, *_ffn_args(p))


def _window_sum(hcat, level, step):
    s = hcat
    for lv in range(level):
        d = step << lv
        s = s[d:] + s[:s.shape[0] - d]
    return s


def _poolffn_kernel(x_ref, pprev_ref, nm_ref, pw_ref, ps_ref, nf_ref, prev_ref, wup_ref, cw_ref, cb_ref, wdown_ref,
                    nfin_ref, y_ref, pnew_ref, new_ref, pcarry_ref, carry_ref, act_ref, mix_ref, *, lay, start_pos):
    x = lay.load(x_ref)
    rows = x.shape[0]
    h = _rms(x, nm_ref[...])
    row = lax.broadcasted_iota(jnp.int32, (rows, 1), 0)
    if lay.nseq:
        step = lay.nseq
        pos = start_pos + row // step
        hcat = jnp.concatenate([lay.load_state(pprev_ref), h], axis=0)
        lay.store_state(pnew_ref, hcat[rows:])
        first = POOL_BUF * step
    else:
        step = 1
        pos = start_pos + pl.program_id(1) * rows + row
        if lay.nt > 1:
            @pl.when(pl.program_id(1) == 0)
            def _():
                pcarry_ref[...] = pprev_ref[...]
            pprev = pcarry_ref[...]
        else:
            pprev = pprev_ref[...]
        hcat = jnp.concatenate([pprev, h], axis=0)
        pnew_ref[...] = hcat[rows:]
        if lay.nt > 1:
            pcarry_ref[...] = hcat[rows:]
        first = 2 * SUB
    for gi, win in enumerate(POOL_WINDOWS):
        cols = slice(gi * POOL_GW, (gi + 1) * POOL_GW)
        if lay.nseq:
            lo = first - (win - 1) * step
            wsum = _window_sum(hcat[:, cols], gi + 1, step)[lo:lo + rows]
        else:
            s = hcat[:, cols]
            for lv in range(gi + 1):
                s = s + pltpu.roll(s, 1 << lv, 0)
            wsum = s[first:]
        cnt = jnp.minimum(win, pos + 1).astype(F32)
        pooled = wsum / cnt - h[:, cols]
        mix_ref[:, cols] = _dot(pooled.astype(BF16), pw_ref[gi])
    x3 = x + mix_ref[...] * ps_ref[...]
    hf = _rms(x3, nf_ref[...]).astype(BF16)
    f = _conv_ffn(hf, lay, prev_ref, carry_ref, new_ref, wup_ref, cw_ref, cb_ref, wdown_ref, act_ref)
    lay.store(y_ref, _rms(x3 + f, nfin_ref[...]))


def _poolffn(x, pool_prev, prev, p, *, n, nb, nt, rows, lay, start_pos):
    row_spec = functools.partial(lay.row_spec, rows)
    return pl.pallas_call(
        functools.partial(_poolffn_kernel, lay=lay, start_pos=start_pos),
        grid=(nb, nt),
        in_specs=[row_spec(D_MODEL), lay.state_spec(pool_prev, nb),
                  _layer_spec(p["norm_mix"], 1), _layer_spec(p["pool_w"], 0), _const_spec((1, D_MODEL)),
                  _layer_spec(p["norm_ffn"], 1)] + _ffn_specs(lay, prev, nb, p, 1) + [_const_spec((1, D_MODEL))],
        out_specs=[row_spec(D_MODEL), lay.new_state(pool_prev, nb)[0], lay.new_state(prev, nb)[0]],
        out_shape=[lay.row_shape(n, D_MODEL), lay.new_state(pool_prev, nb)[1], lay.new_state(prev, nb)[1]],
        scratch_shapes=[pltpu.VMEM((2 * SUB, D_MODEL), F32), pltpu.VMEM((SUB, 2 * D_FF), F32),
                        pltpu.VMEM((rows, D_FF), BF16), pltpu.VMEM((rows, D_MODEL), F32)],
        compiler_params=_PARAMS_2D,
        name="pool_ffn",
    )(x, pool_prev, p["norm_mix"], p["pool_w"], p["pool_scale"], p["norm_ffn"], prev, *_ffn_args(p),
      p["norm_final"])


def _trunk(x, n_seq, seq_len, conv_prev, s0, pool_prev, ffn_prev, start_pos, p, shared_state=False):
    n = n_seq * seq_len
    if seq_len == SUB:
        rows = min(SHORT_ROW_TILE, n)
        geom = dict(n=n, nb=n // rows, nt=1, rows=rows, lay=_Layout(rows // SUB, 1))
    else:
        rows = min(ROW_TILE, seq_len)
        geom = dict(n=n, nb=n_seq, nt=seq_len // rows, rows=rows, lay=_Layout(0, seq_len // rows, shared_state))
    tiled = seq_len % SEQ_ROWS == 0 and n_seq % PAIR == 0
    q, k, v, z, gates, conv_new = _proj(x, conv_prev, p, BF16 if tiled else F32, **geom)
    if seq_len == SUB:
        flat = lambda a: a.reshape(n, a.shape[-1])
        o, s_new = _delta_short(flat(q), flat(k), flat(v), flat(gates), s0, nseq=min(SHORT_SEQS, n_seq))
        o = o.reshape(n_seq, SUB, VAL)
    elif tiled:
        o, s_new = _delta_tiles(q, k, v, gates, s0, n_seq=n_seq, seq_len=seq_len)
    else:
        o, s_new = _delta_long(q, k, v, gates, s0, chunk=min(GDN_CHUNK, seq_len), **geom)
    x2, ffn_new0 = _mixffn(x, o, z, ffn_prev[0], p, **geom)
    y, pool_new, ffn_new1 = _poolffn(x2, pool_prev, ffn_prev[1], p, start_pos=start_pos, **geom)
    return y, conv_new, s_new, pool_new, (ffn_new0, ffn_new1)


def _prep_params(norm_mix, norm_ffn, gdn_w_in, gdn_conv_w, gdn_A_log, gdn_dt_bias, gdn_norm_w, gdn_w_out,
                 pool_w, pool_scale, ffn_w_up, ffn_conv_w, ffn_conv_b, ffn_w_down, norm_final):
    gate_pad = ((0, 0), (0, LANE - 2 * HEADS))
    head_pad = ((0, 0), (HEADS, LANE - 2 * HEADS))
    row = lambda a: a.reshape(a.shape[0], 1, a.shape[1])
    return {
        "norm_mix": row(norm_mix), "norm_ffn": row(norm_ffn),
        "w_in": gdn_w_in.astype(BF16),
        "w_g": jnp.pad(gdn_w_in[0][:, QKV + VAL:], gate_pad).astype(BF16),
        "gdn_conv_w": gdn_conv_w,
        "a_log": jnp.pad(gdn_A_log[0:1], head_pad),
        "dt_bias": jnp.pad(gdn_dt_bias[0:1], head_pad),
        "gdn_norm_w": gdn_norm_w[0:1],
        "w_out": gdn_w_out.astype(BF16),
        "pool_w": pool_w.astype(BF16),
        "pool_scale": pool_scale[0:1],
        "w_up": ffn_w_up.astype(BF16), "ffn_conv_w": ffn_conv_w, "ffn_conv_b": row(ffn_conv_b),
        "w_down": ffn_w_down.astype(BF16),
        "norm_final": norm_final[None],
    }


def kernel(x_prompt, x_sample, state_gdn_conv, state_gdn_rec, state_pool, state_ffn_conv, meta_tokens, norm_mix, norm_ffn, gdn_w_in, gdn_conv_w, gdn_A_log, gdn_dt_bias, gdn_norm_w, gdn_w_out, pool_w, pool_scale, ffn_w_up, ffn_conv_w, ffn_conv_b, ffn_w_down, norm_final):
    p = _prep_params(norm_mix, norm_ffn, gdn_w_in, gdn_conv_w, gdn_A_log, gdn_dt_bias, gdn_norm_w, gdn_w_out,
                     pool_w, pool_scale, ffn_w_up, ffn_conv_w, ffn_conv_b, ffn_w_down, norm_final)
    bp, lp, _ = x_prompt.shape
    bs, ls, _ = x_sample.shape

    _, m_conv, m_s, m_pool, m_ffn = _trunk(
        meta_tokens.astype(F32), 1, N_META,
        jnp.zeros((SUB, QKV), F32), jnp.zeros((1, HEADS, DK, DV), F32), jnp.zeros((2 * SUB, D_MODEL), F32),
        (jnp.zeros((SUB, 2 * D_FF), F32), jnp.zeros((SUB, 2 * D_FF), F32)), 0, p)

    yp, p_conv, p_s, p_pool, p_ffn = _trunk(
        x_prompt.reshape(bp * lp, D_MODEL), bp, lp, m_conv, m_s, m_pool, m_ffn, N_META, p, shared_state=True)

    tm = lambda a: jnp.swapaxes(a, 0, 1)
    ys, s_conv, s_s, s_pool, s_ffn = _trunk(
        x_sample, bs, ls,
        tm(state_gdn_conv[0]), state_gdn_rec[0], tm(state_pool[0]),
        (tm(state_ffn_conv[0]), tm(state_ffn_conv[1])), PAST_LEN, p)

    last = lambda a, b, r: a.reshape(b, -1, a.shape[-1])[:, -r:]
    return (yp.reshape(bp, lp, D_MODEL), ys,
            last(p_conv, bp, CONV_W - 1)[None], p_s[None], last(p_pool, bp, POOL_BUF)[None],
            jnp.stack([last(p_ffn[0], bp, FFN_CONV_W - 1), last(p_ffn[1], bp, FFN_CONV_W - 1)]),
            tm(s_conv)[None], s_s[None], tm(s_pool)[None],
            jnp.stack([tm(s_ffn[0]), tm(s_ffn[1])]))
```
